```python
import math
import jax, jax.numpy as jnp
from jax import lax
import numpy as np

D_MODEL = 1024
BATCH = 8
SEQ = 2048
DEPTH = 1
DEC_BATCH = 16
DEC_SEQ = 4096
PAST_LEN = 128

D_INNER = 2 * D_MODEL
HEAD_DIM = 64
SSD_HEADS = D_INNER // HEAD_DIM
SSD_GROUPS = 8
HEADS_PER_GROUP = SSD_HEADS // SSD_GROUPS
D_STATE = 128
CONV_WIDTH = 7
CONV_DIM = D_INNER + 2 * SSD_GROUPS * D_STATE
CHUNK = 128
NORM_GROUP = D_INNER // SSD_GROUPS
D_FOURIER = D_MODEL
FOURIER_GROUP_DIM = 128
FOURIER_GROUPS = D_FOURIER // FOURIER_GROUP_DIM
N_IN = D_INNER + CONV_DIM + 2 * SSD_HEADS + D_FOURIER + 2 * D_MODEL
D_FF = -(-8 * D_MODEL // (3 * 256)) * 256
EPS = 1e-5

kernel_name = 'hybrid_bidir_ssd_fnet_block'


def _rmsnorm(x, g):
    xf = x.astype(jnp.float32)
    y = xf * lax.rsqrt(jnp.mean(xf * xf, axis=-1, keepdims=True) + EPS)
    return (y * g.astype(jnp.float32)).astype(x.dtype)


def _centred_dwconv(u, w, b):
    pad = CONV_WIDTH // 2
    y = lax.conv_general_dilated(u, w[:, None, :].astype(u.dtype), window_strides=(1,),
                                 padding=[(pad, pad)], dimension_numbers=('NWC', 'WIO', 'NWC'),
                                 feature_group_count=u.shape[-1])
    return y + b


def _ssd_single(xh, dt, bm, cm, a):
    s = xh.shape[0]
    c = s // CHUNK
    xdt = (xh * dt[..., None]).reshape(c, CHUNK, SSD_GROUPS, HEADS_PER_GROUP, HEAD_DIM)
    cum = jnp.cumsum((dt * a).reshape(c, CHUNK, SSD_GROUPS, HEADS_PER_GROUP), axis=1)
    bc = bm.reshape(c, CHUNK, SSD_GROUPS, D_STATE)
    cc = cm.reshape(c, CHUNK, SSD_GROUPS, D_STATE)
    lower = jnp.tril(jnp.ones((CHUNK, CHUNK), dtype=bool))[None, :, :, None, None]
    seg = cum[:, :, None] - cum[:, None, :]
    decay = jnp.exp(jnp.where(lower, seg, -jnp.inf))
    scores = jnp.einsum('clgn,csgn->clsg', cc, bc)
    y_diag = jnp.einsum('clsg,clsgr,csgrp->clgrp', scores, decay, xdt)
    to_end = jnp.exp(cum[:, -1:] - cum)
    states = jnp.einsum('clgn,clgr,clgrp->cgrpn', bc, to_end, xdt)
    chunk_decay = jnp.exp(cum[:, -1])

    def step(h, inp):
        st, dec = inp
        return h * dec[..., None, None] + st, h

    h0 = jnp.zeros((SSD_GROUPS, HEADS_PER_GROUP, HEAD_DIM, D_STATE), jnp.float32)
    _, h_in = lax.scan(step, h0, (states, chunk_decay))
    y_off = jnp.einsum('clgn,cgrpn,clgr->clgrp', cc, h_in, jnp.exp(cum))
    return (y_diag + y_off).reshape(s, SSD_GROUPS, HEADS_PER_GROUP, HEAD_DIM)


def _ssd(xs, dt, bm, cm, a):
    return lax.map(lambda t: _ssd_single(t[0], t[1], t[2], t[3], a), (xs, dt, bm, cm))


def _block(x, norm_mix, w_in, conv_w, conv_b, dt_bias_f, dt_bias_b, a_log_f, a_log_b, d_skip,
           ssd_norm, w_ssd_out, w_fourier_out, b_fourier_out, w_out, norm_ffn, w_gate_up, w_down):
    bsz, s, _ = x.shape
    f32 = jnp.float32
    h = _rmsnorm(x, norm_mix)
    proj = h @ w_in
    o = 0
    z = proj[..., o:o + D_INNER]; o += D_INNER
    xbc = proj[..., o:o + CONV_DIM]; o += CONV_DIM
    dt_raw = proj[..., o:o + 2 * SSD_HEADS]; o += 2 * SSD_HEADS
    u = proj[..., o:o + D_FOURIER]; o += D_FOURIER
    gate_logits = proj[..., o:o + 2 * D_MODEL]

    xbc = jax.nn.silu(_centred_dwconv(xbc, conv_w, conv_b)).astype(f32)
    gn = SSD_GROUPS * D_STATE
    xs = xbc[..., :D_INNER].reshape(bsz, s, SSD_GROUPS, HEADS_PER_GROUP, HEAD_DIM)
    bm = xbc[..., D_INNER:D_INNER + gn].reshape(bsz, s, SSD_GROUPS, D_STATE)
    cm = xbc[..., D_INNER + gn:].reshape(bsz, s, SSD_GROUPS, D_STATE)
    dt = jax.nn.softplus(dt_raw.astype(f32) + jnp.concatenate([dt_bias_f, dt_bias_b]).astype(f32))
    dt_f = dt[..., :SSD_HEADS].reshape(bsz, s, SSD_GROUPS, HEADS_PER_GROUP)
    dt_b = dt[..., SSD_HEADS:].reshape(bsz, s, SSD_GROUPS, HEADS_PER_GROUP)
    a_f = -jnp.exp(a_log_f.astype(f32)).reshape(SSD_GROUPS, HEADS_PER_GROUP)
    a_b = -jnp.exp(a_log_b.astype(f32)).reshape(SSD_GROUPS, HEADS_PER_GROUP)
    y_f = _ssd(xs, dt_f, bm, cm, a_f)
    rev = lambda t: jnp.flip(t, axis=1)
    y_b = rev(_ssd(rev(xs), rev(dt_b), rev(bm), rev(cm), a_b))
    y = y_f + y_b + d_skip.astype(f32).reshape(SSD_GROUPS, HEADS_PER_GROUP)[..., None] * xs
    y = y.reshape(bsz, s, D_INNER) * jax.nn.silu(z.astype(f32))
    yg = y.reshape(bsz, s, SSD_GROUPS, NORM_GROUP)
    yg = yg * lax.rsqrt(jnp.mean(yg * yg, axis=-1, keepdims=True) + EPS)
    y = (yg.reshape(bsz, s, D_INNER) * ssd_norm.astype(f32)).astype(x.dtype)
    a_out = y @ w_ssd_out

    uf = u.astype(f32).reshape(bsz, s, FOURIER_GROUPS, FOURIER_GROUP_DIM)
    mixed = jnp.fft.fft2(uf, axes=(1, 3), norm='ortho').real
    mixed = mixed.reshape(bsz, s, D_FOURIER).astype(x.dtype)
    f_out = mixed @ w_fourier_out + b_fourier_out

    gates = jax.nn.sigmoid(gate_logits.astype(f32))
    merged = (gates[..., :D_MODEL] * a_out.astype(f32) + gates[..., D_MODEL:] * f_out.astype(f32)).astype(x.dtype)
    x = x + merged @ w_out

    h2 = _rmsnorm(x, norm_ffn)
    gu = h2 @ w_gate_up
    x = x + (jax.nn.silu(gu[..., :D_FF]) * gu[..., D_FF:]) @ w_down
    return x


def _trunk(x, layer_params, norm_final):
    for i in range(DEPTH):
        x = _block(x, *[p[i] for p in layer_params])
    return _rmsnorm(x, norm_final)


def _dt_bias_init(k, shape):
    u = jax.random.uniform(k, shape, jnp.float32)
    dt = jnp.exp(u * (math.log(0.1) - math.log(0.001)) + math.log(0.001))
    return dt + jnp.log(-jnp.expm1(-dt))


def setup_inputs(seed: int = 0) -> dict:
    key = jax.random.key(seed)
    ks = jax.random.split(key, 20)
    nrm = lambda k, shape, scale: jax.random.normal(k, shape, jnp.float32) * scale
    L = DEPTH
    return {
        'x_prompt': nrm(ks[0], (BATCH, SEQ, D_MODEL), 1.0),
        'x_sample': nrm(ks[1], (DEC_BATCH, DEC_SEQ, D_MODEL), 1.0),
        'norm_mix': 1.0 + nrm(ks[2], (L, D_MODEL), 0.01),
        'w_in': nrm(ks[3], (L, D_MODEL, N_IN), D_MODEL ** -0.5),
        'conv_w': nrm(ks[4], (L, CONV_WIDTH, CONV_DIM), CONV_WIDTH ** -0.5),
        'conv_b': nrm(ks[5], (L, CONV_DIM), 0.01),
        'dt_bias_f': _dt_bias_init(ks[6], (L, SSD_HEADS)),
        'dt_bias_b': _dt_bias_init(ks[7], (L, SSD_HEADS)),
        'a_log_f': jnp.log(jax.random.uniform(ks[8], (L, SSD_HEADS), jnp.float32, 1.0, 16.0)),
        'a_log_b': jnp.log(jax.random.uniform(ks[9], (L, SSD_HEADS), jnp.float32, 1.0, 16.0)),
        'd_skip': 1.0 + nrm(ks[10], (L, SSD_HEADS), 0.01),
        'ssd_norm': 1.0 + nrm(ks[11], (L, D_INNER), 0.01),
        'w_ssd_out': nrm(ks[12], (L, D_INNER, D_MODEL), D_INNER ** -0.5),
        'w_fourier_out': nrm(ks[13], (L, D_FOURIER, D_MODEL), D_FOURIER ** -0.5),
        'b_fourier_out': nrm(ks[14], (L, D_MODEL), 0.01),
        'w_out': nrm(ks[15], (L, D_MODEL, D_MODEL), D_MODEL ** -0.5),
        'norm_ffn': 1.0 + nrm(ks[16], (L, D_MODEL), 0.01),
        'w_gate_up': nrm(ks[17], (L, D_MODEL, 2 * D_FF), D_MODEL ** -0.5),
        'w_down': nrm(ks[18], (L, D_FF, D_MODEL), D_FF ** -0.5),
        'norm_final': 1.0 + nrm(ks[19], (D_MODEL,), 0.01),
    }


def reference(x_prompt, x_sample, norm_mix, w_in, conv_w, conv_b, dt_bias_f, dt_bias_b, a_log_f, a_log_b,
              d_skip, ssd_norm, w_ssd_out, w_fourier_out, b_fourier_out, w_out, norm_ffn, w_gate_up, w_down,
              norm_final):
    layer_params = (norm_mix, w_in, conv_w, conv_b, dt_bias_f, dt_bias_b, a_log_f, a_log_b, d_skip,
                    ssd_norm, w_ssd_out, w_fourier_out, b_fourier_out, w_out, norm_ffn, w_gate_up, w_down)
    y_prompt = _trunk(x_prompt, layer_params, norm_final)
    y_sample = _trunk(x_sample, layer_params, norm_final)
    return (y_prompt, y_sample)
```

```python
import functools
import math

import numpy as np
import jax
import jax.numpy as jnp
from jax import lax
from jax.experimental import pallas as pl
from jax.experimental.pallas import tpu as pltpu

F32 = jnp.float32
BF16 = jnp.bfloat16

D_MODEL = 1024
D_INNER = 2048
HEAD_DIM = 64
SSD_HEADS = 32
SSD_GROUPS = 8
HEADS_PER_GROUP = 4
D_STATE = 128
CONV_WIDTH = 7
CONV_PAD = CONV_WIDTH // 2
CHUNK = 128
GROUP_CH = D_INNER // SSD_GROUPS
D_FOURIER = 1024
FOURIER_GROUP_DIM = 128
FOURIER_GROUPS = 8
D_FF = 2816
EPS = 1e-5

LANES = 128
SUBLANES = 8
DFT_S1 = 64
PROJ_TILE = 1024
N_MAIN = 8192
VMEM_LIMIT = 56 * 1024 * 1024


def _sigmoid(v):
    return 1.0 / (1.0 + jnp.exp(-v))


def _softplus(v):
    return jnp.maximum(v, 0.0) + jnp.log1p(jnp.exp(-jnp.abs(v)))


def _rms_scale(v):
    return v * lax.rsqrt(jnp.mean(v * v, axis=-1, keepdims=True) + EPS)


def _dot(a, b):
    return jnp.dot(a, b, preferred_element_type=F32)


def _inproj_kernel(x_ref, g_ref, w_ref, wdt_ref, cs_ref, proj_ref, zr_ref, zi_ref, dtt_ref,
                   h_ref, *, n_tiles):
    j = pl.program_id(1)

    @pl.when(j == 0)
    def _():
        h = (_rms_scale(x_ref[...]) * g_ref[...]).astype(BF16)
        h_ref[...] = h
        dtt_ref[...] = _dot(h, wdt_ref[...]).T

    @pl.when(j < n_tiles - 1)
    def _():
        proj_ref[...] = _dot(h_ref[...], w_ref[...]).astype(BF16)

    @pl.when(j == n_tiles - 1)
    def _():
        u = _dot(h_ref[...], w_ref[...]).astype(BF16)
        for g in range(FOURIER_GROUPS):
            sl = slice(g * FOURIER_GROUP_DIM, (g + 1) * FOURIER_GROUP_DIM)
            z = _dot(u[:, sl], cs_ref[...])
            zr_ref[:, sl] = z[:, :FOURIER_GROUP_DIM].astype(BF16)
            zi_ref[:, sl] = z[:, FOURIER_GROUP_DIM:].astype(BF16)


def _inproj(x2d, norm_g, w_all, w_dt, cs, bm):
    t = x2d.shape[0]
    n_tiles = w_all.shape[1] // PROJ_TILE
    last = n_tiles - 2
    return pl.pallas_call(
        functools.partial(_inproj_kernel, n_tiles=n_tiles),
        grid=(t // bm, n_tiles),
        in_specs=[
            pl.BlockSpec((bm, D_MODEL), lambda i, j: (i, 0)),
            pl.BlockSpec((1, D_MODEL), lambda i, j: (0, 0)),
            pl.BlockSpec((D_MODEL, PROJ_TILE), lambda i, j: (0, j)),
            pl.BlockSpec((D_MODEL, LANES), lambda i, j: (0, 0)),
            pl.BlockSpec((FOURIER_GROUP_DIM, 2 * FOURIER_GROUP_DIM), lambda i, j: (0, 0)),
        ],
        out_specs=[
            pl.BlockSpec((bm, PROJ_TILE), lambda i, j: (i, jnp.minimum(j, last))),
            pl.BlockSpec((bm, D_FOURIER), lambda i, j: (i, 0)),
            pl.BlockSpec((bm, D_FOURIER), lambda i, j: (i, 0)),
            pl.BlockSpec((LANES, bm), lambda i, j: (0, i)),
        ],
        out_shape=[
            jax.ShapeDtypeStruct((t, N_MAIN), BF16),
            jax.ShapeDtypeStruct((t, D_FOURIER), BF16),
            jax.ShapeDtypeStruct((t, D_FOURIER), BF16),
            jax.ShapeDtypeStruct((LANES, t), F32),
        ],
        scratch_shapes=[pltpu.VMEM((bm, D_MODEL), BF16)],
        compiler_params=pltpu.CompilerParams(
            dimension_semantics=("arbitrary", "arbitrary"), vmem_limit_bytes=VMEM_LIMIT),
        name="inproj",
    )(x2d, norm_g, w_all, w_dt, cs)


def _ssd_kernel(xs_ref, b_ref, c_ref, z_ref, dtt_ref,
                cwx_ref, cwb_ref, cwc_ref, cbx_ref, cbb_ref, cbc_ref,
                bias_ref, alog_ref, dsk_ref, nrm_ref,
                out_ref,
                raw_ref, y_ref, locb_ref, eb_ref, cc_ref, stf_ref, stb_ref, *, seq):
    n_chunks = seq // CHUNK
    pad = SUBLANES

    zpad = jnp.zeros((pad, LANES), F32)
    for s in range(4):
        raw_ref[s, 0:pad, :] = zpad
        raw_ref[s, seq + pad:seq + 2 * pad, :] = zpad

    def fill(c, carry):
        t0 = pl.multiple_of(c * CHUNK, CHUNK)
        xv = xs_ref[pl.ds(t0, CHUNK), :].astype(F32)
        raw_ref[0, pl.ds(t0 + pad, CHUNK), :] = xv[:, :LANES]
        raw_ref[1, pl.ds(t0 + pad, CHUNK), :] = xv[:, LANES:]
        raw_ref[2, pl.ds(t0 + pad, CHUNK), :] = b_ref[pl.ds(t0, CHUNK), :].astype(F32)
        raw_ref[3, pl.ds(t0 + pad, CHUNK), :] = c_ref[pl.ds(t0, CHUNK), :].astype(F32)
        return carry

    lax.fori_loop(0, n_chunks, fill, 0)

    def conv_silu(slab, t0, w, bias):
        acc = bias
        for k in range(CONV_WIDTH):
            acc = acc + w[k:k + 1, :] * raw_ref[slab, pl.ds(t0 + (pad - CONV_PAD) + k, CHUNK), :]
        return acc * _sigmoid(acc)

    cwx = cwx_ref[...]
    cbx = cbx_ref[...]
    cwb = cwb_ref[...]
    cbb = cbb_ref[...]
    cwc = cwc_ref[...]
    cbc = cbc_ref[...]
    bias = bias_ref[...]
    a_neg = -jnp.exp(alog_ref[...])
    dskip = dsk_ref[...]

    row8 = lax.broadcasted_iota(jnp.int32, (SUBLANES, LANES), 0)
    lane8 = lax.broadcasted_iota(jnp.int32, (SUBLANES, LANES), 1)
    li = lax.broadcasted_iota(jnp.int32, (CHUNK, CHUNK), 0)
    si = lax.broadcasted_iota(jnp.int32, (CHUNK, CHUNK), 1)
    lane_g = lax.broadcasted_iota(jnp.int32, (CHUNK, GROUP_CH), 1)
    is_fwd_row = row8 < HEADS_PER_GROUP

    def expand_heads(cols):
        lo = jnp.where(si < HEAD_DIM, cols[0], cols[1])
        hi = jnp.where(si < HEAD_DIM, cols[2], cols[3])
        return jnp.concatenate([lo, hi], axis=1)

    stf_ref[...] = jnp.zeros((D_STATE, GROUP_CH), F32)
    stb_ref[...] = jnp.zeros((D_STATE, GROUP_CH), F32)

    def fwd(c, carry):
        t0 = pl.multiple_of(c * CHUNK, CHUNK)
        xsv = jnp.concatenate([conv_silu(0, t0, cwx[:, :LANES], cbx[:, :LANES]),
                               conv_silu(1, t0, cwx[:, LANES:], cbx[:, LANES:])], axis=1)
        bv = conv_silu(2, t0, cwb, cbb)
        cv = conv_silu(3, t0, cwc, cbc)
        xb = xsv.astype(BF16)
        cb = cv.astype(BF16)
        cc_ref[pl.ds(t0, CHUNK), :] = cb
        bt = bv.T
        scores = _dot(cb, bt.astype(BF16))

        dt = _softplus(dtt_ref[:, pl.ds(t0, CHUNK)] + bias)
        dta = dt * a_neg
        cf = dta
        rb = dta
        k = 1
        while k < CHUNK:
            cf = cf + jnp.where(lane8 >= k, pltpu.roll(cf, k, 1), 0.0)
            rb = rb + jnp.where(lane8 < CHUNK - k, pltpu.roll(rb, CHUNK - k, 1), 0.0)
            k *= 2
        cum = jnp.where(is_fwd_row, cf, rb)
        tot = jnp.where(is_fwd_row, cum[:, CHUNK - 1:CHUNK], cum[:, 0:1])
        wrow = dt * jnp.exp(tot - cum)
        erow = jnp.exp(cum)
        q = jnp.concatenate([cum, erow, jnp.zeros((CHUNK - 2 * SUBLANES, LANES), F32)], axis=0)
        qt = q.T

        lhs_m = []
        lhs_f = []
        lhs_b = []
        xm = []
        for h in range(HEADS_PER_GROUP):
            hb = HEADS_PER_GROUP + h
            arg = jnp.where(li >= si,
                            qt[:, h:h + 1] - cum[h:h + 1, :],
                            qt[:, hb:hb + 1] - cum[hb:hb + 1, :])
            coef = jnp.where(li > si, dt[h:h + 1, :],
                             jnp.where(li < si, dt[hb:hb + 1, :], dt[h:h + 1, :] + dt[hb:hb + 1, :]))
            lhs_m.append((scores * jnp.exp(arg) * coef).astype(BF16))
            lhs_f.append((bt * wrow[h:h + 1, :]).astype(BF16))
            lhs_b.append((bt * wrow[hb:hb + 1, :]).astype(BF16))
            in_head = (lane_g >= h * HEAD_DIM) & (lane_g < (h + 1) * HEAD_DIM)
            xm.append(jnp.where(in_head, xb, jnp.zeros_like(xb)))
        lhs = jnp.concatenate([jnp.concatenate(lhs_m, axis=1),
                               jnp.concatenate(lhs_f, axis=1),
                               jnp.concatenate(lhs_b, axis=1)], axis=0)
        big = _dot(lhs, jnp.concatenate(xm, axis=0))

        ef = expand_heads([qt[:, SUBLANES + h:SUBLANES + h + 1] for h in range(HEADS_PER_GROUP)])
        eb = expand_heads([qt[:, SUBLANES + HEADS_PER_GROUP + h:SUBLANES + HEADS_PER_GROUP + h + 1]
                           for h in range(HEADS_PER_GROUP)])
        y_off = _dot(cb, stf_ref[...].astype(BF16)) * ef
        y_ref[pl.ds(t0, CHUNK), :] = big[:CHUNK] + y_off + dskip * xsv
        stf_ref[...] = stf_ref[...] * ef[CHUNK - 1:CHUNK, :] + big[CHUNK:2 * CHUNK]
        locb_ref[c] = big[2 * CHUNK:]
        eb_ref[pl.ds(t0, CHUNK), :] = eb
        return carry

    lax.fori_loop(0, n_chunks, fwd, 0)

    nrm = nrm_ref[...]

    def bwd(i, carry):
        c = n_chunks - 1 - i
        t0 = pl.multiple_of(c * CHUNK, CHUNK)
        eb = eb_ref[pl.ds(t0, CHUNK), :]
        y = y_ref[pl.ds(t0, CHUNK), :] + _dot(cc_ref[pl.ds(t0, CHUNK), :], stb_ref[...].astype(BF16)) * eb
        stb_ref[...] = stb_ref[...] * eb[0:1, :] + locb_ref[c]
        zz = z_ref[pl.ds(t0, CHUNK), :].astype(F32)
        y = y * (zz * _sigmoid(zz))
        out_ref[pl.ds(t0, CHUNK), :] = (_rms_scale(y) * nrm).astype(BF16)
        return carry

    lax.fori_loop(0, n_chunks, bwd, 0)


def _ssd(proj3, dtt, conv_w, conv_b, bias_rep, alog_rep, dskip_exp, ssd_norm):
    bsz, seq, _ = proj3.shape
    n_chunks = seq // CHUNK
    xs0 = D_INNER // GROUP_CH
    b0 = 2 * D_INNER // D_STATE
    c0 = b0 + SSD_GROUPS
    cwb0 = D_INNER // D_STATE
    cwc0 = cwb0 + SSD_GROUPS
    row_spec = lambda width, off: pl.BlockSpec((None, seq, width), lambda b, g: (b, 0, off + g))
    par_spec = lambda rows, width, off: pl.BlockSpec((rows, width), lambda b, g: (0, off + g))
    return pl.pallas_call(
        functools.partial(_ssd_kernel, seq=seq),
        grid=(bsz, SSD_GROUPS),
        in_specs=[
            row_spec(GROUP_CH, xs0), row_spec(D_STATE, b0), row_spec(D_STATE, c0), row_spec(GROUP_CH, 0),
            pl.BlockSpec((SUBLANES, seq), lambda b, g: (g, b)),
            par_spec(CONV_WIDTH, GROUP_CH, 0), par_spec(CONV_WIDTH, D_STATE, cwb0),
            par_spec(CONV_WIDTH, D_STATE, cwc0),
            par_spec(1, GROUP_CH, 0), par_spec(1, D_STATE, cwb0), par_spec(1, D_STATE, cwc0),
            pl.BlockSpec((SUBLANES, LANES), lambda b, g: (g, 0)),
            pl.BlockSpec((SUBLANES, LANES), lambda b, g: (g, 0)),
            par_spec(1, GROUP_CH, 0), par_spec(1, GROUP_CH, 0),
        ],
        out_specs=pl.BlockSpec((None, seq, GROUP_CH), lambda b, g: (b, 0, g)),
        out_shape=jax.ShapeDtypeStruct((bsz, seq, D_INNER), BF16),
        scratch_shapes=[
            pltpu.VMEM((4, seq + 2 * SUBLANES, LANES), F32),
            pltpu.VMEM((seq, GROUP_CH), F32),
            pltpu.VMEM((n_chunks, D_STATE, GROUP_CH), F32),
            pltpu.VMEM((seq, GROUP_CH), F32),
            pltpu.VMEM((seq, D_STATE), BF16),
            pltpu.VMEM((D_STATE, GROUP_CH), F32),
            pltpu.VMEM((D_STATE, GROUP_CH), F32),
        ],
        compiler_params=pltpu.CompilerParams(
            dimension_semantics=("arbitrary", "arbitrary"), vmem_limit_bytes=VMEM_LIMIT),
        name="ssd",
    )(proj3, proj3, proj3, proj3, dtt, conv_w, conv_w, conv_w, conv_b, conv_b, conv_b,
      bias_rep, alog_rep, dskip_exp, ssd_norm)


def _dft_a_kernel(zr_ref, zi_ref, g_ref, ar_ref, ai_ref, *, t2_per_step):
    for tt in range(t2_per_step):
        sl = slice(tt * D_FOURIER, (tt + 1) * D_FOURIER)
        z = jnp.concatenate([zr_ref[:, sl], zi_ref[:, sl]], axis=0)
        a = _dot(g_ref[tt], z)
        ar_ref[tt] = a[:DFT_S1].astype(BF16)
        ai_ref[tt] = a[DFT_S1:].astype(BF16)


def _dft_a(zr, zi, gmat, bsz, seq):
    s2 = seq // DFT_S1
    t2s = min(8, s2)
    zr3 = zr.reshape(bsz, DFT_S1, s2 * D_FOURIER)
    zi3 = zi.reshape(bsz, DFT_S1, s2 * D_FOURIER)
    in_spec = pl.BlockSpec((None, DFT_S1, t2s * D_FOURIER), lambda b, j: (b, 0, j))
    out_spec = pl.BlockSpec((None, t2s, DFT_S1, D_FOURIER), lambda b, j: (b, j, 0, 0))
    out_sds = jax.ShapeDtypeStruct((bsz, s2, DFT_S1, D_FOURIER), BF16)
    return pl.pallas_call(
        functools.partial(_dft_a_kernel, t2_per_step=t2s),
        grid=(bsz, s2 // t2s),
        in_specs=[in_spec, in_spec,
                  pl.BlockSpec((t2s, 2 * DFT_S1, 2 * DFT_S1), lambda b, j: (j, 0, 0))],
        out_specs=[out_spec, out_spec],
        out_shape=[out_sds, out_sds],
        compiler_params=pltpu.CompilerParams(
            dimension_semantics=("arbitrary", "arbitrary"), vmem_limit_bytes=VMEM_LIMIT),
        name="dft_a",
    )(zr3, zi3, gmat)


def _dft_b_kernel(ar_ref, ai_ref, f_ref, o_ref):
    a = jnp.concatenate([ar_ref[...], ai_ref[...]], axis=0)
    o_ref[...] = _dot(f_ref[...], a).astype(BF16)


def _dft_b(ar, ai, fmat, bsz, seq):
    s2 = seq // DFT_S1
    cols = DFT_S1 * D_FOURIER
    cb = 4096
    ar3 = ar.reshape(bsz, s2, cols)
    ai3 = ai.reshape(bsz, s2, cols)
    spec = pl.BlockSpec((None, s2, cb), lambda b, j: (b, 0, j))
    out = pl.pallas_call(
        _dft_b_kernel,
        grid=(bsz, cols // cb),
        in_specs=[spec, spec, pl.BlockSpec((s2, 2 * s2), lambda b, j: (0, 0))],
        out_specs=spec,
        out_shape=jax.ShapeDtypeStruct((bsz, s2, cols), BF16),
        compiler_params=pltpu.CompilerParams(
            dimension_semantics=("arbitrary", "arbitrary"), vmem_limit_bytes=VMEM_LIMIT),
        name="dft_b",
    )(ar3, ai3, fmat)
    return out.reshape(bsz * seq, D_FOURIER)


def _dft_constants(seq):
    s1, s2 = DFT_S1, seq // DFT_S1
    cidx = np.arange(FOURIER_GROUP_DIM)
    ang_c = 2.0 * np.pi * np.outer(cidx, cidx) / FOURIER_GROUP_DIM
    cs = np.concatenate([np.cos(ang_c), -np.sin(ang_c)], axis=1)
    k1 = np.arange(s1)
    t2 = np.arange(s2)
    ang_a = 2.0 * np.pi * (np.outer(k1, k1)[None] / s1 + (np.outer(t2, k1) / seq)[:, :, None])
    gr, gi = np.cos(ang_a), -np.sin(ang_a)
    gmat = np.concatenate([np.concatenate([gr, -gi], axis=2),
                           np.concatenate([gi, gr], axis=2)], axis=1)
    ang_b = 2.0 * np.pi * np.outer(t2, t2) / s2
    scale = 1.0 / math.sqrt(seq * FOURIER_GROUP_DIM)
    fmat = np.concatenate([np.cos(ang_b), np.sin(ang_b)], axis=1) * scale
    to_bf16 = lambda a: jnp.asarray(a.astype(np.float32)).astype(BF16)
    return to_bf16(cs), to_bf16(gmat), to_bf16(fmat)


def _merge_kernel(y_ref, mx_ref, gl_ref, x_ref, wssd_ref, wf_ref, bf_ref, wo_ref, o_ref):
    a_out = _dot(y_ref[...], wssd_ref[...])
    f_out = _dot(mx_ref[...], wf_ref[...]) + bf_ref[...]
    gates = _sigmoid(gl_ref[...].astype(F32))
    merged = (gates[:, :D_MODEL] * a_out + gates[:, D_MODEL:] * f_out).astype(BF16)
    o_ref[...] = x_ref[...] + _dot(merged, wo_ref[...])


def _merge(y2d, mixed, proj, x2d, w_ssd, w_f, b_f, w_o, bm):
    t = x2d.shape[0]
    gate_blk = (N_MAIN - 2 * D_MODEL) // (2 * D_MODEL)
    const = lambda r, c: pl.BlockSpec((r, c), lambda i: (0, 0))
    return pl.pallas_call(
        _merge_kernel,
        grid=(t // bm,),
        in_specs=[
            pl.BlockSpec((bm, D_INNER), lambda i: (i, 0)),
            pl.BlockSpec((bm, D_FOURIER), lambda i: (i, 0)),
            pl.BlockSpec((bm, 2 * D_MODEL), lambda i: (i, gate_blk)),
            pl.BlockSpec((bm, D_MODEL), lambda i: (i, 0)),
            const(D_INNER, D_MODEL), const(D_FOURIER, D_MODEL), const(1, D_MODEL), const(D_MODEL, D_MODEL),
        ],
        out_specs=pl.BlockSpec((bm, D_MODEL), lambda i: (i, 0)),
        out_shape=jax.ShapeDtypeStruct((t, D_MODEL), F32),
        compiler_params=pltpu.CompilerParams(
            dimension_semantics=("arbitrary",), vmem_limit_bytes=VMEM_LIMIT),
        name="merge",
    )(y2d, mixed, proj, x2d, w_ssd, w_f, b_f, w_o)


def _ffn_kernel(x_ref, g_ref, wg_ref, wu_ref, wd_ref, gfin_ref, o_ref, h_ref, acc_ref, *, n_ff):
    f = pl.program_id(1)

    @pl.when(f == 0)
    def _():
        h_ref[...] = (_rms_scale(x_ref[...]) * g_ref[...]).astype(BF16)
        acc_ref[...] = jnp.zeros_like(acc_ref)

    h = h_ref[...]
    gate = _dot(h, wg_ref[...])
    up = _dot(h, wu_ref[...])
    act = (gate * _sigmoid(gate) * up).astype(BF16)
    acc_ref[...] += _dot(act, wd_ref[...])

    @pl.when(f == n_ff - 1)
    def _():
        o_ref[...] = _rms_scale(x_ref[...] + acc_ref[...]) * gfin_ref[...]


def _ffn(x1, norm_g, w_gate, w_up, w_down, norm_fin, bm, ff_tile):
    t = x1.shape[0]
    n_ff = D_FF // ff_tile
    return pl.pallas_call(
        functools.partial(_ffn_kernel, n_ff=n_ff),
        grid=(t // bm, n_ff),
        in_specs=[
            pl.BlockSpec((bm, D_MODEL), lambda i, f: (i, 0)),
            pl.BlockSpec((1, D_MODEL), lambda i, f: (0, 0)),
            pl.BlockSpec((D_MODEL, ff_tile), lambda i, f: (0, f)),
            pl.BlockSpec((D_MODEL, ff_tile), lambda i, f: (0, f)),
            pl.BlockSpec((ff_tile, D_MODEL), lambda i, f: (f, 0)),
            pl.BlockSpec((1, D_MODEL), lambda i, f: (0, 0)),
        ],
        out_specs=pl.BlockSpec((bm, D_MODEL), lambda i, f: (i, 0)),
        out_shape=jax.ShapeDtypeStruct((t, D_MODEL), F32),
        scratch_shapes=[pltpu.VMEM((bm, D_MODEL), BF16), pltpu.VMEM((bm, D_MODEL), F32)],
        compiler_params=pltpu.CompilerParams(
            dimension_semantics=("arbitrary", "arbitrary"), vmem_limit_bytes=VMEM_LIMIT),
        name="ffn",
    )(x1, norm_g, w_gate, w_up, w_down, norm_fin)


def _prep_weights(norm_mix, w_in, conv_w, conv_b, dt_bias_f, dt_bias_b, a_log_f, a_log_b, d_skip,
                  ssd_norm, w_ssd_out, w_fourier_out, b_fourier_out, w_out, norm_ffn, w_gate_up,
                  w_down, norm_final):
    o_dt = D_INNER + (D_INNER + 2 * SSD_GROUPS * D_STATE)
    o_u = o_dt + 2 * SSD_HEADS
    o_g = o_u + D_FOURIER
    w_all = jnp.concatenate([w_in[:, :o_dt], w_in[:, o_g:], w_in[:, o_u:o_g]], axis=1).astype(BF16)

    def by_group(f, b):
        return jnp.concatenate([f.reshape(SSD_GROUPS, HEADS_PER_GROUP),
                                b.reshape(SSD_GROUPS, HEADS_PER_GROUP)], axis=1).reshape(-1)

    w_dt_cols = w_in[:, o_dt:o_u]
    w_dt = jnp.concatenate([w_dt_cols[:, :SSD_HEADS].reshape(D_MODEL, SSD_GROUPS, HEADS_PER_GROUP),
                            w_dt_cols[:, SSD_HEADS:].reshape(D_MODEL, SSD_GROUPS, HEADS_PER_GROUP)],
                           axis=2).reshape(D_MODEL, 2 * SSD_HEADS)
    w_dt = jnp.pad(w_dt, ((0, 0), (0, LANES - 2 * SSD_HEADS))).astype(BF16)
    rep = lambda v: jnp.broadcast_to(v.astype(F32)[:, None], (2 * SSD_HEADS, LANES))
    return dict(
        norm_mix=norm_mix.reshape(1, D_MODEL), w_all=w_all, w_dt=w_dt,
        conv_w=conv_w, conv_b=conv_b.reshape(1, -1),
        bias_rep=rep(by_group(dt_bias_f, dt_bias_b)), alog_rep=rep(by_group(a_log_f, a_log_b)),
        dskip_exp=jnp.repeat(d_skip.astype(F32), HEAD_DIM).reshape(1, D_INNER),
        ssd_norm=ssd_norm.reshape(1, D_INNER),
        w_ssd=w_ssd_out.astype(BF16), w_f=w_fourier_out.astype(BF16),
        b_f=b_fourier_out.reshape(1, D_MODEL), w_o=w_out.astype(BF16),
        norm_ffn=norm_ffn.reshape(1, D_MODEL),
        w_gate=w_gate_up[:, :D_FF].astype(BF16), w_up=w_gate_up[:, D_FF:].astype(BF16),
        w_down=w_down.astype(BF16), norm_final=norm_final.reshape(1, D_MODEL),
    )


def _trunk(x, p):
    bsz, seq, _ = x.shape
    t = bsz * seq
    bm = min(1024, t)
    x2d = x.reshape(t, D_MODEL)
    cs, gmat, fmat = _dft_constants(seq)
    proj, zr, zi, dtt = _inproj(x2d, p["norm_mix"], p["w_all"], p["w_dt"], cs, bm)
    y = _ssd(proj.reshape(bsz, seq, N_MAIN), dtt, p["conv_w"], p["conv_b"], p["bias_rep"],
             p["alog_rep"], p["dskip_exp"], p["ssd_norm"])
    ar, ai = _dft_a(zr, zi, gmat, bsz, seq)
    mixed = _dft_b(ar, ai, fmat, bsz, seq)
    x1 = _merge(y.reshape(t, D_INNER), mixed, proj, x2d, p["w_ssd"], p["w_f"], p["b_f"], p["w_o"],
                min(512, t))
    out = _ffn(x1, p["norm_ffn"], p["w_gate"], p["w_up"], p["w_down"], p["norm_final"], bm, D_FF // 2)
    return out.reshape(bsz, seq, D_MODEL)


def kernel(x_prompt, x_sample, norm_mix, w_in, conv_w, conv_b, dt_bias_f, dt_bias_b, a_log_f, a_log_b,
           d_skip, ssd_norm, w_ssd_out, w_fourier_out, b_fourier_out, w_out, norm_ffn, w_gate_up, w_down,
           norm_final):
    p = _prep_weights(norm_mix[0], w_in[0], conv_w[0], conv_b[0], dt_bias_f[0], dt_bias_b[0],
                      a_log_f[0], a_log_b[0], d_skip[0], ssd_norm[0], w_ssd_out[0], w_fourier_out[0],
                      b_fourier_out[0], w_out[0], norm_ffn[0], w_gate_up[0], w_down[0], norm_final)
    return (_trunk(x_prompt, p), _trunk(x_sample, p))
```

```python
import functools
import math

import numpy as np
import jax
import jax.numpy as jnp
from jax import lax
from jax.experimental import pallas as pl
from jax.experimental.pallas import tpu as pltpu

F32 = jnp.float32
BF16 = jnp.bfloat16

D_MODEL = 1024
D_INNER = 2048
HEAD_DIM = 64
SSD_HEADS = 32
SSD_GROUPS = 8
HEADS_PER_GROUP = 4
D_STATE = 128
CONV_WIDTH = 7
CONV_PAD = CONV_WIDTH // 2
CHUNK = 128
GROUP_CH = D_INNER // SSD_GROUPS
D_FOURIER = 1024
FOURIER_GROUP_DIM = 128
FOURIER_GROUPS = 8
D_FF = 2816
EPS = 1e-5

LANES = 128
SUBLANES = 8
BF16_ROWS = 16
DFT_S1 = 64
PROJ_TILE = 1024
N_MAIN = 8192
VMEM_LIMIT = 56 * 1024 * 1024


def _sigmoid(v):
    return 1.0 / (1.0 + jnp.exp(-v))


def _softplus(v):
    return jnp.maximum(v, 0.0) + jnp.log1p(jnp.exp(-jnp.abs(v)))


def _rms_scale(v):
    return v * lax.rsqrt(jnp.mean(v * v, axis=-1, keepdims=True) + EPS)


def _dot(a, b):
    return jnp.dot(a, b, preferred_element_type=F32)


def _inproj_kernel(x_ref, g_ref, w_ref, wdt_ref, cs_ref, proj_ref, zr_ref, zi_ref, dtt_ref,
                   h_ref, *, n_tiles):
    j = pl.program_id(1)

    @pl.when(j == 0)
    def _():
        h = (_rms_scale(x_ref[...]) * g_ref[...]).astype(BF16)
        h_ref[...] = h
        dtt_ref[...] = _dot(h, wdt_ref[...]).T

    @pl.when(j < n_tiles - 1)
    def _():
        proj_ref[...] = _dot(h_ref[...], w_ref[...]).astype(BF16)

    @pl.when(j == n_tiles - 1)
    def _():
        u = _dot(h_ref[...], w_ref[...]).astype(BF16)
        for g in range(FOURIER_GROUPS):
            sl = slice(g * FOURIER_GROUP_DIM, (g + 1) * FOURIER_GROUP_DIM)
            z = _dot(u[:, sl], cs_ref[...])
            zr_ref[:, sl] = z[:, :FOURIER_GROUP_DIM].astype(BF16)
            zi_ref[:, sl] = z[:, FOURIER_GROUP_DIM:].astype(BF16)


def _inproj(x2d, norm_g, w_all, w_dt, cs, bm):
    t = x2d.shape[0]
    n_tiles = w_all.shape[1] // PROJ_TILE
    last = n_tiles - 2
    return pl.pallas_call(
        functools.partial(_inproj_kernel, n_tiles=n_tiles),
        grid=(t // bm, n_tiles),
        in_specs=[
            pl.BlockSpec((bm, D_MODEL), lambda i, j: (i, 0)),
            pl.BlockSpec((1, D_MODEL), lambda i, j: (0, 0)),
            pl.BlockSpec((D_MODEL, PROJ_TILE), lambda i, j: (0, j)),
            pl.BlockSpec((D_MODEL, LANES), lambda i, j: (0, 0)),
            pl.BlockSpec((FOURIER_GROUP_DIM, 2 * FOURIER_GROUP_DIM), lambda i, j: (0, 0)),
        ],
        out_specs=[
            pl.BlockSpec((bm, PROJ_TILE), lambda i, j: (i, jnp.minimum(j, last))),
            pl.BlockSpec((bm, D_FOURIER), lambda i, j: (i, 0)),
            pl.BlockSpec((bm, D_FOURIER), lambda i, j: (i, 0)),
            pl.BlockSpec((LANES, bm), lambda i, j: (0, i)),
        ],
        out_shape=[
            jax.ShapeDtypeStruct((t, N_MAIN), BF16),
            jax.ShapeDtypeStruct((t, D_FOURIER), BF16),
            jax.ShapeDtypeStruct((t, D_FOURIER), BF16),
            jax.ShapeDtypeStruct((LANES, t), F32),
        ],
        scratch_shapes=[pltpu.VMEM((bm, D_MODEL), BF16)],
        compiler_params=pltpu.CompilerParams(
            dimension_semantics=("arbitrary", "arbitrary"), vmem_limit_bytes=VMEM_LIMIT),
        name="inproj",
    )(x2d, norm_g, w_all, w_dt, cs)


Q_CUM = 0
Q_ONE = 32
Q_E = 48
N_PIECES = 3
HEAD_DIRS = 2 * HEADS_PER_GROUP


def _split3(v):
    p0 = v.astype(BF16).astype(F32)
    r1 = v - p0
    p1 = r1.astype(BF16).astype(F32)
    return p0, p1, (r1 - p1).astype(BF16).astype(F32)


def _ssd_kernel(xs_ref, b_ref, c_ref, z_ref, dtt_ref,
                cwx_ref, cwb_ref, cwc_ref, cbx_ref, cbb_ref, cbc_ref,
                bias_ref, alog_ref, dsk_ref, nrm_ref, sel_ref, exp_ref,
                out_ref,
                raw_ref, y_ref, xsb_ref, bt_ref, cc_ref, locf_ref, locb_ref, sinf_ref, sinb_ref,
                qt_ref, rows_ref, cdf_ref, cdb_ref, stf_ref, stb_ref, *, seq):
    n_chunks = seq // CHUNK
    pad = SUBLANES

    def per_chunk_pair(fn):
        def body(i, carry):
            fn(2 * i)
            fn(2 * i + 1)
            return carry
        lax.fori_loop(0, n_chunks // 2, body, 0)

    zpad = jnp.zeros((pad, LANES), F32)
    for s in range(4):
        raw_ref[s, 0:pad, :] = zpad
        raw_ref[s, seq + pad:seq + 2 * pad, :] = zpad

    def fill(c):
        t0 = pl.multiple_of(c * CHUNK, CHUNK)
        xv = xs_ref[pl.ds(t0, CHUNK), :].astype(F32)
        raw_ref[0, pl.ds(t0 + pad, CHUNK), :] = xv[:, :LANES]
        raw_ref[1, pl.ds(t0 + pad, CHUNK), :] = xv[:, LANES:]
        raw_ref[2, pl.ds(t0 + pad, CHUNK), :] = b_ref[pl.ds(t0, CHUNK), :].astype(F32)
        raw_ref[3, pl.ds(t0 + pad, CHUNK), :] = c_ref[pl.ds(t0, CHUNK), :].astype(F32)

    per_chunk_pair(fill)

    row_s = lax.broadcasted_iota(jnp.int32, (HEAD_DIRS, seq), 0)
    lane_s = lax.broadcasted_iota(jnp.int32, (HEAD_DIRS, seq), 1) & (CHUNK - 1)
    dt_all = _softplus(dtt_ref[...] + bias_ref[:, 0:1])
    dta = dt_all * (-jnp.exp(alog_ref[:, 0:1]))
    cf = dta
    rb = dta
    k = 1
    while k < CHUNK:
        cf = cf + jnp.where(lane_s >= k, pltpu.roll(cf, k, 1), 0.0)
        rb = rb + jnp.where(lane_s < CHUNK - k, pltpu.roll(rb, seq - k, 1), 0.0)
        k *= 2
    rows_ref[0] = dt_all
    rows_ref[1] = jnp.where(row_s < HEADS_PER_GROUP, cf, rb)

    def conv_silu(slab, t0, w, bias):
        acc = bias
        for k in range(CONV_WIDTH):
            acc = acc + w[k:k + 1, :] * raw_ref[slab, pl.ds(t0 + (pad - CONV_PAD) + k, CHUNK), :]
        return acc * _sigmoid(acc)

    cwx = cwx_ref[...]
    cbx = cbx_ref[...]
    cwb = cwb_ref[...]
    cbb = cbb_ref[...]
    cwc = cwc_ref[...]
    cbc = cbc_ref[...]
    dskip = dsk_ref[...]

    def conv_pass(c):
        t0 = pl.multiple_of(c * CHUNK, CHUNK)
        xsv = jnp.concatenate([conv_silu(0, t0, cwx[:, :LANES], cbx[:, :LANES]),
                               conv_silu(1, t0, cwx[:, LANES:], cbx[:, LANES:])], axis=1)
        y_ref[pl.ds(t0, CHUNK), :] = dskip * xsv
        xsb_ref[pl.ds(t0, CHUNK), :] = xsv.astype(BF16)
        bt_ref[c] = conv_silu(2, t0, cwb, cbb).T
        cc_ref[pl.ds(t0, CHUNK), :] = conv_silu(3, t0, cwc, cbc).astype(BF16)

    per_chunk_pair(conv_pass)

    row8 = lax.broadcasted_iota(jnp.int32, (SUBLANES, LANES), 0)
    row16 = lax.broadcasted_iota(jnp.int32, (BF16_ROWS, LANES), 0)
    li = lax.broadcasted_iota(jnp.int32, (CHUNK, CHUNK), 0)
    si = lax.broadcasted_iota(jnp.int32, (CHUNK, CHUNK), 1)
    lane_g = lax.broadcasted_iota(jnp.int32, (CHUNK, GROUP_CH), 1)
    is_fwd_row = row8 < HEADS_PER_GROUP
    zero8 = jnp.zeros((SUBLANES, LANES), F32)
    ones_rows = jnp.where(row8 < N_PIECES, 1.0, 0.0)

    def diag_pass(c):
        t0 = pl.multiple_of(c * CHUNK, CHUNK)
        cb = cc_ref[pl.ds(t0, CHUNK), :]
        bt = bt_ref[c]
        xb = xsb_ref[pl.ds(t0, CHUNK), :]
        scores = _dot(cb, bt.astype(BF16))

        dt = rows_ref[0, :, pl.ds(t0, CHUNK)]
        cum = rows_ref[1, :, pl.ds(t0, CHUNK)]
        tot = jnp.where(is_fwd_row, cum[:, CHUNK - 1:CHUNK], cum[:, 0:1])
        wrow = dt * jnp.exp(tot - cum)
        cp = _split3(cum)
        ep = _split3(jnp.exp(cum))

        q = jnp.concatenate([cp[0], cp[1], cp[2], zero8, ones_rows, zero8, ep[0], ep[1], ep[2]]
                            + [zero8] * (CHUNK // SUBLANES - 9), axis=0)
        qtf = q.T
        qt = qtf.astype(BF16)
        qt_ref[c] = qt
        dyn = []
        for j in range(HEAD_DIRS):
            dyn.append(jnp.where(row16 == 0, -cp[0][j:j + 1, :],
                                 jnp.where(row16 == 1, -cp[1][j:j + 1, :],
                                           jnp.where(row16 == 2, -cp[2][j:j + 1, :], 0.0))))
        dyn = jnp.concatenate(dyn, axis=1).astype(BF16)
        rhs = jnp.concatenate([sel_ref[0:Q_ONE, :], dyn, sel_ref[Q_ONE + BF16_ROWS:, :]], axis=0)
        arg_all = _dot(qt, rhs)

        edge = jnp.concatenate([qtf[0:SUBLANES, :], qtf[CHUNK - SUBLANES:, :]], axis=0).astype(BF16)
        cd = _dot(edge, exp_ref[...])
        cdf_ref[pl.ds(c, 1), :] = cd[2 * SUBLANES - 1:2 * SUBLANES, :GROUP_CH]
        cdb_ref[pl.ds(c, 1), :] = cd[0:1, GROUP_CH:]

        lhs_m = []
        lhs_f = []
        lhs_b = []
        xm = []
        for h in range(HEADS_PER_GROUP):
            hb = HEADS_PER_GROUP + h
            arg = jnp.where(li >= si, arg_all[:, h * CHUNK:(h + 1) * CHUNK],
                            arg_all[:, hb * CHUNK:(hb + 1) * CHUNK])
            coef = jnp.where(li > si, dt[h:h + 1, :],
                             jnp.where(li < si, dt[hb:hb + 1, :], dt[h:h + 1, :] + dt[hb:hb + 1, :]))
            lhs_m.append((scores * jnp.exp(arg) * coef).astype(BF16))
            lhs_f.append((bt * wrow[h:h + 1, :]).astype(BF16))
            lhs_b.append((bt * wrow[hb:hb + 1, :]).astype(BF16))
            in_head = (lane_g >= h * HEAD_DIM) & (lane_g < (h + 1) * HEAD_DIM)
            xm.append(jnp.where(in_head, xb, jnp.zeros_like(xb)))
        lhs = jnp.concatenate([jnp.concatenate(lhs_m, axis=1),
                               jnp.concatenate(lhs_f, axis=1),
                               jnp.concatenate(lhs_b, axis=1)], axis=0)
        big = _dot(lhs, jnp.concatenate(xm, axis=0))
        y_ref[pl.ds(t0, CHUNK), :] += big[:CHUNK]
        locf_ref[c] = big[CHUNK:2 * CHUNK]
        locb_ref[c] = big[2 * CHUNK:]

    per_chunk_pair(diag_pass)

    stf_ref[...] = jnp.zeros((D_STATE, GROUP_CH), F32)
    stb_ref[...] = jnp.zeros((D_STATE, GROUP_CH), F32)

    def scan(i, carry):
        j = n_chunks - 1 - i
        sf = stf_ref[...]
        sinf_ref[i] = sf.astype(BF16)
        stf_ref[...] = sf * cdf_ref[pl.ds(i, 1), :] + locf_ref[i]
        sb = stb_ref[...]
        sinb_ref[j] = sb.astype(BF16)
        stb_ref[...] = sb * cdb_ref[pl.ds(j, 1), :] + locb_ref[j]
        return carry

    lax.fori_loop(0, n_chunks, scan, 0)

    nrm = nrm_ref[...]

    def final_pass(c):
        t0 = pl.multiple_of(c * CHUNK, CHUNK)
        st = jnp.concatenate([sinf_ref[c], sinb_ref[c]], axis=1)
        yo = _dot(cc_ref[pl.ds(t0, CHUNK), :], st)
        ee = _dot(qt_ref[c], exp_ref[...])
        y = (y_ref[pl.ds(t0, CHUNK), :] + yo[:, :GROUP_CH] * ee[:, :GROUP_CH]
             + yo[:, GROUP_CH:] * ee[:, GROUP_CH:])
        zz = z_ref[pl.ds(t0, CHUNK), :].astype(F32)
        y = y * (zz * _sigmoid(zz))
        out_ref[pl.ds(t0, CHUNK), :] = (_rms_scale(y) * nrm).astype(BF16)

    per_chunk_pair(final_pass)


def _ssd_constants():
    sel = np.zeros((CHUNK, HEAD_DIRS * CHUNK), np.float32)
    expand = np.zeros((CHUNK, 2 * GROUP_CH), np.float32)
    for r in range(N_PIECES):
        for j in range(HEAD_DIRS):
            sel[Q_CUM + SUBLANES * r + j, j * CHUNK:(j + 1) * CHUNK] = 1.0
            expand[Q_E + SUBLANES * r + j, j * HEAD_DIM:(j + 1) * HEAD_DIM] = 1.0
    return jnp.asarray(sel).astype(BF16), jnp.asarray(expand).astype(BF16)


def _ssd(proj3, dtt, conv_w, conv_b, bias_rep, alog_rep, dskip_exp, ssd_norm):
    bsz, seq, _ = proj3.shape
    n_chunks = seq // CHUNK
    assert n_chunks % 2 == 0
    sel, expand = _ssd_constants()
    xs0 = D_INNER // GROUP_CH
    b0 = 2 * D_INNER // D_STATE
    c0 = b0 + SSD_GROUPS
    cwb0 = D_INNER // D_STATE
    cwc0 = cwb0 + SSD_GROUPS
    row_spec = lambda width, off: pl.BlockSpec((None, seq, width), lambda b, g: (b, 0, off + g))
    par_spec = lambda rows, width, off: pl.BlockSpec((rows, width), lambda b, g: (0, off + g))
    return pl.pallas_call(
        functools.partial(_ssd_kernel, seq=seq),
        grid=(bsz, SSD_GROUPS),
        in_specs=[
            row_spec(GROUP_CH, xs0), row_spec(D_STATE, b0), row_spec(D_STATE, c0), row_spec(GROUP_CH, 0),
            pl.BlockSpec((HEAD_DIRS, seq), lambda b, g: (g, b)),
            par_spec(CONV_WIDTH, GROUP_CH, 0), par_spec(CONV_WIDTH, D_STATE, cwb0),
            par_spec(CONV_WIDTH, D_STATE, cwc0),
            par_spec(1, GROUP_CH, 0), par_spec(1, D_STATE, cwb0), par_spec(1, D_STATE, cwc0),
            pl.BlockSpec((HEAD_DIRS, LANES), lambda b, g: (g, 0)),
            pl.BlockSpec((HEAD_DIRS, LANES), lambda b, g: (g, 0)),
            par_spec(1, GROUP_CH, 0), par_spec(1, GROUP_CH, 0),
            pl.BlockSpec(sel.shape, lambda b, g: (0, 0)),
            pl.BlockSpec(expand.shape, lambda b, g: (0, 0)),
        ],
        out_specs=pl.BlockSpec((None, seq, GROUP_CH), lambda b, g: (b, 0, g)),
        out_shape=jax.ShapeDtypeStruct((bsz, seq, D_INNER), BF16),
        scratch_shapes=[
            pltpu.VMEM((4, seq + 2 * SUBLANES, LANES), F32),
            pltpu.VMEM((seq, GROUP_CH), F32),
            pltpu.VMEM((seq, GROUP_CH), BF16),
            pltpu.VMEM((n_chunks, D_STATE, CHUNK), F32),
            pltpu.VMEM((seq, D_STATE), BF16),
            pltpu.VMEM((n_chunks, D_STATE, GROUP_CH), F32),
            pltpu.VMEM((n_chunks, D_STATE, GROUP_CH), F32),
            pltpu.VMEM((n_chunks, D_STATE, GROUP_CH), BF16),
            pltpu.VMEM((n_chunks, D_STATE, GROUP_CH), BF16),
            pltpu.VMEM((n_chunks, CHUNK, CHUNK), BF16),
            pltpu.VMEM((2, HEAD_DIRS, seq), F32),
            pltpu.VMEM((n_chunks, GROUP_CH), F32),
            pltpu.VMEM((n_chunks, GROUP_CH), F32),
            pltpu.VMEM((D_STATE, GROUP_CH), F32),
            pltpu.VMEM((D_STATE, GROUP_CH), F32),
        ],
        compiler_params=pltpu.CompilerParams(
            dimension_semantics=("arbitrary", "arbitrary"), vmem_limit_bytes=VMEM_LIMIT),
        name="ssd",
    )(proj3, proj3, proj3, proj3, dtt, conv_w, conv_w, conv_w, conv_b, conv_b, conv_b,
      bias_rep, alog_rep, dskip_exp, ssd_norm, sel, expand)


def _dft_a_kernel(zr_ref, zi_ref, g_ref, ar_ref, ai_ref, *, t2_per_step):
    for tt in range(t2_per_step):
        sl = slice(tt * D_FOURIER, (tt + 1) * D_FOURIER)
        z = jnp.concatenate([zr_ref[:, sl], zi_ref[:, sl]], axis=0)
        a = _dot(g_ref[tt], z)
        ar_ref[tt] = a[:DFT_S1].astype(BF16)
        ai_ref[tt] = a[DFT_S1:].astype(BF16)


def _dft_a(zr, zi, gmat, bsz, seq):
    s2 = seq // DFT_S1
    t2s = min(8, s2)
    zr3 = zr.reshape(bsz, DFT_S1, s2 * D_FOURIER)
    zi3 = zi.reshape(bsz, DFT_S1, s2 * D_FOURIER)
    in_spec = pl.BlockSpec((None, DFT_S1, t2s * D_FOURIER), lambda b, j: (b, 0, j))
    out_spec = pl.BlockSpec((None, t2s, DFT_S1, D_FOURIER), lambda b, j: (b, j, 0, 0))
    out_sds = jax.ShapeDtypeStruct((bsz, s2, DFT_S1, D_FOURIER), BF16)
    return pl.pallas_call(
        functools.partial(_dft_a_kernel, t2_per_step=t2s),
        grid=(bsz, s2 // t2s),
        in_specs=[in_spec, in_spec,
                  pl.BlockSpec((t2s, 2 * DFT_S1, 2 * DFT_S1), lambda b, j: (j, 0, 0))],
        out_specs=[out_spec, out_spec],
        out_shape=[out_sds, out_sds],
        compiler_params=pltpu.CompilerParams(
            dimension_semantics=("arbitrary", "arbitrary"), vmem_limit_bytes=VMEM_LIMIT),
        name="dft_a",
    )(zr3, zi3, gmat)


def _dft_b_kernel(ar_ref, ai_ref, f_ref, o_ref):
    a = jnp.concatenate([ar_ref[...], ai_ref[...]], axis=0)
    o_ref[...] = _dot(f_ref[...], a).astype(BF16)


def _dft_b(ar, ai, fmat, bsz, seq):
    s2 = seq // DFT_S1
    cols = DFT_S1 * D_FOURIER
    cb = 4096
    ar3 = ar.reshape(bsz, s2, cols)
    ai3 = ai.reshape(bsz, s2, cols)
    spec = pl.BlockSpec((None, s2, cb), lambda b, j: (b, 0, j))
    out = pl.pallas_call(
        _dft_b_kernel,
        grid=(bsz, cols // cb),
        in_specs=[spec, spec, pl.BlockSpec((s2, 2 * s2), lambda b, j: (0, 0))],
        out_specs=spec,
        out_shape=jax.ShapeDtypeStruct((bsz, s2, cols), BF16),
        compiler_params=pltpu.CompilerParams(
            dimension_semantics=("arbitrary", "arbitrary"), vmem_limit_bytes=VMEM_LIMIT),
        name="dft_b",
    )(ar3, ai3, fmat)
    return out.reshape(bsz * seq, D_FOURIER)


def _dft_constants(seq):
    s1, s2 = DFT_S1, seq // DFT_S1
    cidx = np.arange(FOURIER_GROUP_DIM)
    ang_c = 2.0 * np.pi * np.outer(cidx, cidx) / FOURIER_GROUP_DIM
    cs = np.concatenate([np.cos(ang_c), -np.sin(ang_c)], axis=1)
    k1 = np.arange(s1)
    t2 = np.arange(s2)
    ang_a = 2.0 * np.pi * (np.outer(k1, k1)[None] / s1 + (np.outer(t2, k1) / seq)[:, :, None])
    gr, gi = np.cos(ang_a), -np.sin(ang_a)
    gmat = np.concatenate([np.concatenate([gr, -gi], axis=2),
                           np.concatenate([gi, gr], axis=2)], axis=1)
    ang_b = 2.0 * np.pi * np.outer(t2, t2) / s2
    scale = 1.0 / math.sqrt(seq * FOURIER_GROUP_DIM)
    fmat = np.concatenate([np.cos(ang_b), np.sin(ang_b)], axis=1) * scale
    to_bf16 = lambda a: jnp.asarray(a.astype(np.float32)).astype(BF16)
    return to_bf16(cs), to_bf16(gmat), to_bf16(fmat)


def _merge_kernel(y_ref, mx_ref, gl_ref, x_ref, wssd_ref, wf_ref, bf_ref, wo_ref, o_ref):
    a_out = _dot(y_ref[...], wssd_ref[...])
    f_out = _dot(mx_ref[...], wf_ref[...]) + bf_ref[...]
    gates = _sigmoid(gl_ref[...].astype(F32))
    merged = (gates[:, :D_MODEL] * a_out + gates[:, D_MODEL:] * f_out).astype(BF16)
    o_ref[...] = x_ref[...] + _dot(merged, wo_ref[...])


def _merge(y2d, mixed, proj, x2d, w_ssd, w_f, b_f, w_o, bm):
    t = x2d.shape[0]
    gate_blk = (N_MAIN - 2 * D_MODEL) // (2 * D_MODEL)
    const = lambda r, c: pl.BlockSpec((r, c), lambda i: (0, 0))
    return pl.pallas_call(
        _merge_kernel,
        grid=(t // bm,),
        in_specs=[
            pl.BlockSpec((bm, D_INNER), lambda i: (i, 0)),
            pl.BlockSpec((bm, D_FOURIER), lambda i: (i, 0)),
            pl.BlockSpec((bm, 2 * D_MODEL), lambda i: (i, gate_blk)),
            pl.BlockSpec((bm, D_MODEL), lambda i: (i, 0)),
            const(D_INNER, D_MODEL), const(D_FOURIER, D_MODEL), const(1, D_MODEL), const(D_MODEL, D_MODEL),
        ],
        out_specs=pl.BlockSpec((bm, D_MODEL), lambda i: (i, 0)),
        out_shape=jax.ShapeDtypeStruct((t, D_MODEL), F32),
        compiler_params=pltpu.CompilerParams(
            dimension_semantics=("arbitrary",), vmem_limit_bytes=VMEM_LIMIT),
        name="merge",
    )(y2d, mixed, proj, x2d, w_ssd, w_f, b_f, w_o)


def _ffn_kernel(x_ref, g_ref, wg_ref, wu_ref, wd_ref, gfin_ref, o_ref, h_ref, acc_ref, *, n_ff):
    f = pl.program_id(1)

    @pl.when(f == 0)
    def _():
        h_ref[...] = (_rms_scale(x_ref[...]) * g_ref[...]).astype(BF16)
        acc_ref[...] = jnp.zeros_like(acc_ref)

    h = h_ref[...]
    gate = _dot(h, wg_ref[...])
    up = _dot(h, wu_ref[...])
    act = (gate * _sigmoid(gate) * up).astype(BF16)
    acc_ref[...] += _dot(act, wd_ref[...])

    @pl.when(f == n_ff - 1)
    def _():
        o_ref[...] = _rms_scale(x_ref[...] + acc_ref[...]) * gfin_ref[...]


def _ffn(x1, norm_g, w_gate, w_up, w_down, norm_fin, bm, ff_tile):
    t = x1.shape[0]
    n_ff = D_FF // ff_tile
    return pl.pallas_call(
        functools.partial(_ffn_kernel, n_ff=n_ff),
        grid=(t // bm, n_ff),
        in_specs=[
            pl.BlockSpec((bm, D_MODEL), lambda i, f: (i, 0)),
            pl.BlockSpec((1, D_MODEL), lambda i, f: (0, 0)),
            pl.BlockSpec((D_MODEL, ff_tile), lambda i, f: (0, f)),
            pl.BlockSpec((D_MODEL, ff_tile), lambda i, f: (0, f)),
            pl.BlockSpec((ff_tile, D_MODEL), lambda i, f: (f, 0)),
            pl.BlockSpec((1, D_MODEL), lambda i, f: (0, 0)),
        ],
        out_specs=pl.BlockSpec((bm, D_MODEL), lambda i, f: (i, 0)),
        out_shape=jax.ShapeDtypeStruct((t, D_MODEL), F32),
        scratch_shapes=[pltpu.VMEM((bm, D_MODEL), BF16), pltpu.VMEM((bm, D_MODEL), F32)],
        compiler_params=pltpu.CompilerParams(
            dimension_semantics=("arbitrary", "arbitrary"), vmem_limit_bytes=VMEM_LIMIT),
        name="ffn",
    )(x1, norm_g, w_gate, w_up, w_down, norm_fin)


def _prep_weights(norm_mix, w_in, conv_w, conv_b, dt_bias_f, dt_bias_b, a_log_f, a_log_b, d_skip,
                  ssd_norm, w_ssd_out, w_fourier_out, b_fourier_out, w_out, norm_ffn, w_gate_up,
                  w_down, norm_final):
    o_dt = D_INNER + (D_INNER + 2 * SSD_GROUPS * D_STATE)
    o_u = o_dt + 2 * SSD_HEADS
    o_g = o_u + D_FOURIER
    w_all = jnp.concatenate([w_in[:, :o_dt], w_in[:, o_g:], w_in[:, o_u:o_g]], axis=1).astype(BF16)

    def by_group(f, b):
        return jnp.concatenate([f.reshape(SSD_GROUPS, HEADS_PER_GROUP),
                                b.reshape(SSD_GROUPS, HEADS_PER_GROUP)], axis=1).reshape(-1)

    w_dt_cols = w_in[:, o_dt:o_u]
    w_dt = jnp.concatenate([w_dt_cols[:, :SSD_HEADS].reshape(D_MODEL, SSD_GROUPS, HEADS_PER_GROUP),
                            w_dt_cols[:, SSD_HEADS:].reshape(D_MODEL, SSD_GROUPS, HEADS_PER_GROUP)],
                           axis=2).reshape(D_MODEL, 2 * SSD_HEADS)
    w_dt = jnp.pad(w_dt, ((0, 0), (0, LANES - 2 * SSD_HEADS))).astype(BF16)
    rep = lambda v: jnp.broadcast_to(v.astype(F32)[:, None], (2 * SSD_HEADS, LANES))
    return dict(
        norm_mix=norm_mix.reshape(1, D_MODEL), w_all=w_all, w_dt=w_dt,
        conv_w=conv_w, conv_b=conv_b.reshape(1, -1),
        bias_rep=rep(by_group(dt_bias_f, dt_bias_b)), alog_rep=rep(by_group(a_log_f, a_log_b)),
        dskip_exp=jnp.repeat(d_skip.astype(F32), HEAD_DIM).reshape(1, D_INNER),
        ssd_norm=ssd_norm.reshape(1, D_INNER),
        w_ssd=w_ssd_out.astype(BF16), w_f=w_fourier_out.astype(BF16),
        b_f=b_fourier_out.reshape(1, D_MODEL), w_o=w_out.astype(BF16),
        norm_ffn=norm_ffn.reshape(1, D_MODEL),
        w_gate=w_gate_up[:, :D_FF].astype(BF16), w_up=w_gate_up[:, D_FF:].astype(BF16),
        w_down=w_down.astype(BF16), norm_final=norm_final.reshape(1, D_MODEL),
    )


def _trunk(x, p):
    bsz, seq, _ = x.shape
    t = bsz * seq
    bm = min(1024, t)
    x2d = x.reshape(t, D_MODEL)
    cs, gmat, fmat = _dft_constants(seq)
    proj, zr, zi, dtt = _inproj(x2d, p["norm_mix"], p["w_all"], p["w_dt"], cs, bm)
    y = _ssd(proj.reshape(bsz, seq, N_MAIN), dtt, p["conv_w"], p["conv_b"], p["bias_rep"],
             p["alog_rep"], p["dskip_exp"], p["ssd_norm"])
    ar, ai = _dft_a(zr, zi, gmat, bsz, seq)
    mixed = _dft_b(ar, ai, fmat, bsz, seq)
    x1 = _merge(y.reshape(t, D_INNER), mixed, proj, x2d, p["w_ssd"], p["w_f"], p["b_f"], p["w_o"],
                min(512, t))
    out = _ffn(x1, p["norm_ffn"], p["w_gate"], p["w_up"], p["w_down"], p["norm_final"], bm, D_FF // 2)
    return out.reshape(bsz, seq, D_MODEL)


def kernel(x_prompt, x_sample, norm_mix, w_in, conv_w, conv_b, dt_bias_f, dt_bias_b, a_log_f, a_log_b,
           d_skip, ssd_norm, w_ssd_out, w_fourier_out, b_fourier_out, w_out, norm_ffn, w_gate_up, w_down,
           norm_final):
    p = _prep_weights(norm_mix[0], w_in[0], conv_w[0], conv_b[0], dt_bias_f[0], dt_bias_b[0],
                      a_log_f[0], a_log_b[0], d_skip[0], ssd_norm[0], w_ssd_out[0], w_fourier_out[0],
                      b_fourier_out[0], w_out[0], norm_ffn[0], w_gate_up[0], w_down[0], norm_final)
    return (_trunk(x_prompt, p), _trunk(x_sample, p))
```

```python
import functools
import math

import numpy as np
import jax
import jax.numpy as jnp
from jax import lax
from jax.experimental import pallas as pl
from jax.experimental.pallas import tpu as pltpu

F32 = jnp.float32
BF16 = jnp.bfloat16

D_MODEL = 1024
D_INNER = 2048
HEAD_DIM = 64
SSD_HEADS = 32
SSD_GROUPS = 8
HEADS_PER_GROUP = 4
D_STATE = 128
CONV_WIDTH = 7
CONV_PAD = CONV_WIDTH // 2
CHUNK = 128
GROUP_CH = D_INNER // SSD_GROUPS
D_FOURIER = 1024
FOURIER_GROUP_DIM = 128
FOURIER_GROUPS = 8
D_FF = 2816
EPS = 1e-5

LANES = 128
SUBLANES = 8
BF16_ROWS = 16
DFT_N1 = 16
DFT_COLS = 256
PROJ_TILE = 1024
N_MAIN = 8192
VMEM_LIMIT = 56 * 1024 * 1024


def _sigmoid(v):
    return 1.0 / (1.0 + jnp.exp(-v))


def _softplus(v):
    return jnp.maximum(v, 0.0) + jnp.log1p(jnp.exp(-jnp.abs(v)))


def _rms_scale(v):
    return v * lax.rsqrt(jnp.mean(v * v, axis=-1, keepdims=True) + EPS)


def _dot(a, b):
    return jnp.dot(a, b, preferred_element_type=F32)


TILE_XS = 2
TILE_B = 4
TILE_C = 5
TILE_GATE = 6
TILE_FOURIER = 8
HALO = BF16_ROWS


def _inproj_kernel(x_ref, xp_ref, xn_ref, g_ref, w_ref, wdt_ref, cs_ref, cw_ref, cb_ref,
                   zs_ref, xsc_ref, bt_ref, cc_ref, gs_ref, zr_ref, zi_ref, dtt_ref,
                   h_ref, slab_ref, conv_ref, *, tiles_per_seq):
    i = pl.program_id(0)
    j = pl.program_id(1)
    bm = x_ref.shape[0]
    main = pl.ds(HALO, bm)

    @pl.when(j == 0)
    def _():
        g = g_ref[...]
        h = (_rms_scale(x_ref[...]) * g).astype(BF16)
        h_ref[main, :] = h
        pos = i % tiles_per_seq
        hp = _rms_scale(xp_ref[0]) * g
        hn = _rms_scale(xn_ref[0]) * g
        h_ref[0:HALO, :] = jnp.where(pos == 0, 0.0, hp).astype(BF16)
        h_ref[HALO + bm:, :] = jnp.where(pos == tiles_per_seq - 1, 0.0, hn).astype(BF16)
        dtt_ref[...] = _dot(h, wdt_ref[...]).T

    @pl.when(j < TILE_XS)
    def _():
        z = _dot(h_ref[main, :], w_ref[...])
        zs_ref[...] = (z * _sigmoid(z)).astype(BF16)

    @pl.when((j >= TILE_XS) & (j < TILE_GATE))
    def _():
        n_q = PROJ_TILE // (2 * LANES)

        def project(q):
            pe = _dot(h_ref[...], w_ref[:, q * 2 * LANES:(q + 1) * 2 * LANES])
            slab_ref[2 * q] = pe[:, :LANES]
            slab_ref[2 * q + 1] = pe[:, LANES:]

        def conv(q):
            for s in range(2 * q, 2 * q + 2):
                sl = slice(s * LANES, (s + 1) * LANES)
                acc = cb_ref[:, sl]
                for k in range(CONV_WIDTH):
                    acc = acc + cw_ref[k:k + 1, sl] * slab_ref[s, HALO - CONV_PAD + k:HALO - CONV_PAD + k + bm, :]
                conv_ref[:, sl] = (acc * _sigmoid(acc)).astype(BF16)

        project(0)
        project(1)
        for q in range(n_q):
            conv(q)
            if q + 2 < n_q:
                project(q + 2)

    @pl.when((j >= TILE_XS) & (j < TILE_B))
    def _():
        xsc_ref[...] = conv_ref[...]

    @pl.when(j == TILE_B)
    def _():
        for g in range(SSD_GROUPS):
            for c in range(bm // CHUNK):
                rows = slice(c * CHUNK, (c + 1) * CHUNK)
                blk = conv_ref[rows, g * D_STATE:(g + 1) * D_STATE].astype(F32)
                bt_ref[g, rows, :] = blk.T.astype(BF16)

    @pl.when(j == TILE_C)
    def _():
        cc_ref[...] = conv_ref[...]

    @pl.when((j >= TILE_GATE) & (j < TILE_FOURIER))
    def _():
        gs_ref[...] = _sigmoid(_dot(h_ref[main, :], w_ref[...])).astype(BF16)

    @pl.when(j == TILE_FOURIER)
    def _():
        u = _dot(h_ref[main, :], w_ref[...]).astype(BF16)
        for g in range(FOURIER_GROUPS):
            sl = slice(g * FOURIER_GROUP_DIM, (g + 1) * FOURIER_GROUP_DIM)
            z = _dot(u[:, sl], cs_ref[...])
            zr_ref[:, sl] = z[:, :FOURIER_GROUP_DIM].astype(BF16)
            zi_ref[:, sl] = z[:, FOURIER_GROUP_DIM:].astype(BF16)


def _inproj(x2d, norm_g, w_all, w_dt, cs, conv_w, conv_b, bm, seq):
    t = x2d.shape[0]
    n_tiles = w_all.shape[1] // PROJ_TILE
    assert n_tiles == TILE_FOURIER + 1 and seq % bm == 0 and bm % HALO == 0
    halo_blocks = bm // HALO
    x_halo = x2d.reshape(t // HALO, HALO, D_MODEL)
    clip = lambda v, lo, hi: jnp.minimum(jnp.maximum(v, lo), hi)
    tok = lambda width, first, count: pl.BlockSpec(
        (bm, width), lambda i, j: (i, clip(j - first, 0, count - 1)))
    bf16_out = lambda cols: jax.ShapeDtypeStruct((t, cols), BF16)
    return pl.pallas_call(
        functools.partial(_inproj_kernel, tiles_per_seq=seq // bm),
        grid=(t // bm, n_tiles),
        in_specs=[
            pl.BlockSpec((bm, D_MODEL), lambda i, j: (i, 0)),
            pl.BlockSpec((1, HALO, D_MODEL), lambda i, j: (jnp.maximum(i * halo_blocks - 1, 0), 0, 0)),
            pl.BlockSpec((1, HALO, D_MODEL),
                         lambda i, j: (jnp.minimum((i + 1) * halo_blocks, t // HALO - 1), 0, 0)),
            pl.BlockSpec((1, D_MODEL), lambda i, j: (0, 0)),
            pl.BlockSpec((D_MODEL, PROJ_TILE), lambda i, j: (0, j)),
            pl.BlockSpec((D_MODEL, LANES), lambda i, j: (0, 0)),
            pl.BlockSpec((FOURIER_GROUP_DIM, 2 * FOURIER_GROUP_DIM), lambda i, j: (0, 0)),
            pl.BlockSpec((CONV_WIDTH, PROJ_TILE), lambda i, j: (0, clip(j - TILE_XS, 0, 3))),
            pl.BlockSpec((1, PROJ_TILE), lambda i, j: (0, clip(j - TILE_XS, 0, 3))),
        ],
        out_specs=[
            tok(PROJ_TILE, 0, 2),
            tok(PROJ_TILE, TILE_XS, 2),
            pl.BlockSpec((SSD_GROUPS, bm, D_STATE), lambda i, j: (0, i, 0)),
            tok(PROJ_TILE, TILE_C, 1),
            tok(PROJ_TILE, TILE_GATE, 2),
            tok(D_FOURIER, TILE_FOURIER, 1),
            tok(D_FOURIER, TILE_FOURIER, 1),
            pl.BlockSpec((LANES, bm), lambda i, j: (0, i)),
        ],
        out_shape=[
            bf16_out(D_INNER), bf16_out(D_INNER),
            jax.ShapeDtypeStruct((SSD_GROUPS, t, D_STATE), BF16),
            bf16_out(SSD_GROUPS * D_STATE), bf16_out(2 * D_MODEL),
            bf16_out(D_FOURIER), bf16_out(D_FOURIER),
            jax.ShapeDtypeStruct((LANES, t), F32),
        ],
        scratch_shapes=[
            pltpu.VMEM((bm + 2 * HALO, D_MODEL), BF16),
            pltpu.VMEM((PROJ_TILE // LANES, bm + 2 * HALO, LANES), F32),
            pltpu.VMEM((bm, PROJ_TILE), BF16),
        ],
        compiler_params=pltpu.CompilerParams(
            dimension_semantics=("arbitrary", "arbitrary"), vmem_limit_bytes=VMEM_LIMIT),
        name="inproj",
    )(x2d, x_halo, x_halo, norm_g, w_all, w_dt, cs, conv_w, conv_b)


Q_E = 48
N_PIECES = 3
HEAD_DIRS = 2 * HEADS_PER_GROUP
SSD_UNROLL = 4


def _split3(v):
    p0 = v.astype(BF16).astype(F32)
    r1 = v - p0
    p1 = r1.astype(BF16).astype(F32)
    return p0, p1, (r1 - p1).astype(BF16).astype(F32)


def _ssd_kernel(xsb_ref, bt_ref, cc_ref, zs_ref, dtt_ref,
                bias_ref, alog_ref, dsk_ref, nrm_ref, exp_ref,
                out_ref,
                y_ref, locf_ref, locb_ref, sinf_ref, sinb_ref,
                qt_ref, rows_ref, cdf_ref, cdb_ref, stf_ref, stb_ref, *, seq):
    n_chunks = seq // CHUNK

    def per_chunk_group(fn, width):
        def body(i, carry):
            for u in range(width):
                fn(width * i + u)
            return carry
        lax.fori_loop(0, n_chunks // width, body, 0)

    row_s = lax.broadcasted_iota(jnp.int32, (HEAD_DIRS, seq), 0)
    lane_s = lax.broadcasted_iota(jnp.int32, (HEAD_DIRS, seq), 1) & (CHUNK - 1)
    dt_all = _softplus(dtt_ref[...] + bias_ref[:, 0:1])
    dta = dt_all * (-jnp.exp(alog_ref[:, 0:1]))
    cf = dta
    rb = dta
    k = 1
    while k < CHUNK:
        cf = cf + jnp.where(lane_s >= k, pltpu.roll(cf, k, 1), 0.0)
        rb = rb + jnp.where(lane_s < CHUNK - k, pltpu.roll(rb, seq - k, 1), 0.0)
        k *= 2
    rows_ref[0] = dt_all
    rows_ref[1] = jnp.where(row_s < HEADS_PER_GROUP, cf, rb)

    dskip = dsk_ref[...]

    row8 = lax.broadcasted_iota(jnp.int32, (SUBLANES, LANES), 0)
    li = lax.broadcasted_iota(jnp.int32, (CHUNK, CHUNK), 0)
    si = lax.broadcasted_iota(jnp.int32, (CHUNK, CHUNK), 1)
    lane_g = lax.broadcasted_iota(jnp.int32, (CHUNK, GROUP_CH), 1)
    is_fwd_row = row8 < HEADS_PER_GROUP
    zero8 = jnp.zeros((SUBLANES, LANES), F32)
    lower_b = jnp.where(li >= si, 1.0, 0.0).astype(BF16)
    upper_b = jnp.where(li <= si, 1.0, 0.0).astype(BF16)

    def diag_pass(c):
        t0 = pl.multiple_of(c * CHUNK, CHUNK)
        cb = cc_ref[pl.ds(t0, CHUNK), :]
        bt = bt_ref[pl.ds(t0, CHUNK), :]
        xb = xsb_ref[pl.ds(t0, CHUNK), :]
        scores = _dot(cb, bt).astype(BF16)

        dt = rows_ref[0, :, pl.ds(t0, CHUNK)]
        cum = rows_ref[1, :, pl.ds(t0, CHUNK)]
        tot = jnp.where(is_fwd_row, cum[:, CHUNK - 1:CHUNK], cum[:, 0:1])
        wrow = dt * jnp.exp(tot - cum)
        ep = _split3(jnp.exp(cum))

        q = jnp.concatenate([cum] + [zero8] * (Q_E // SUBLANES - 1) + [ep[0], ep[1], ep[2]]
                            + [zero8] * ((CHUNK - Q_E) // SUBLANES - N_PIECES), axis=0)
        qtf = q.T
        qt_ref[c] = qtf.astype(BF16)

        edge = jnp.concatenate([qtf[0:SUBLANES, :], qtf[CHUNK - SUBLANES:, :]], axis=0).astype(BF16)
        cd = _dot(edge, exp_ref[...])
        cdf_ref[pl.ds(c, 1), :] = cd[2 * SUBLANES - 1:2 * SUBLANES, :GROUP_CH]
        cdb_ref[pl.ds(c, 1), :] = cd[0:1, GROUP_CH:]

        dtb = dt.astype(BF16)
        wb = wrow.astype(BF16)
        lhs_m = []
        lhs_f = []
        lhs_b = []
        xm = []
        for h in range(HEADS_PER_GROUP):
            hb = HEADS_PER_GROUP + h
            arg = jnp.where(li >= si, qtf[:, h:h + 1] - cum[h:h + 1, :], qtf[:, hb:hb + 1] - cum[hb:hb + 1, :])
            coef = lower_b * dtb[h:h + 1, :] + upper_b * dtb[hb:hb + 1, :]
            lhs_m.append(scores * jnp.exp(arg).astype(BF16) * coef)
            lhs_f.append(bt * wb[h:h + 1, :])
            lhs_b.append(bt * wb[hb:hb + 1, :])
            in_head = (lane_g >= h * HEAD_DIM) & (lane_g < (h + 1) * HEAD_DIM)
            xm.append(jnp.where(in_head, xb, jnp.zeros_like(xb)))
        lhs = jnp.concatenate([jnp.concatenate(lhs_m, axis=1),
                               jnp.concatenate(lhs_f, axis=1),
                               jnp.concatenate(lhs_b, axis=1)], axis=0)
        big = _dot(lhs, jnp.concatenate(xm, axis=0))
        y_ref[pl.ds(t0, CHUNK), :] = big[:CHUNK] + dskip * xb.astype(F32)
        locf_ref[c] = big[CHUNK:2 * CHUNK]
        locb_ref[c] = big[2 * CHUNK:]

    per_chunk_group(diag_pass, SSD_UNROLL)

    stf_ref[...] = jnp.zeros((D_STATE, GROUP_CH), F32)
    stb_ref[...] = jnp.zeros((D_STATE, GROUP_CH), F32)

    def scan(i, carry):
        j = n_chunks - 1 - i
        sf = stf_ref[...]
        sinf_ref[i] = sf.astype(BF16)
        stf_ref[...] = sf * cdf_ref[pl.ds(i, 1), :] + locf_ref[i]
        sb = stb_ref[...]
        sinb_ref[j] = sb.astype(BF16)
        stb_ref[...] = sb * cdb_ref[pl.ds(j, 1), :] + locb_ref[j]
        return carry

    lax.fori_loop(0, n_chunks, scan, 0)

    nrm = nrm_ref[...]

    def final_pass(c):
        t0 = pl.multiple_of(c * CHUNK, CHUNK)
        st = jnp.concatenate([sinf_ref[c], sinb_ref[c]], axis=1)
        yo = _dot(cc_ref[pl.ds(t0, CHUNK), :], st)
        ee = _dot(qt_ref[c], exp_ref[...])
        y = (y_ref[pl.ds(t0, CHUNK), :] + yo[:, :GROUP_CH] * ee[:, :GROUP_CH]
             + yo[:, GROUP_CH:] * ee[:, GROUP_CH:])
        y = y * zs_ref[pl.ds(t0, CHUNK), :].astype(F32)
        out_ref[pl.ds(t0, CHUNK), :] = (_rms_scale(y) * nrm).astype(BF16)

    per_chunk_group(final_pass, SSD_UNROLL)


def _expand_matrix():
    expand = np.zeros((CHUNK, 2 * GROUP_CH), np.float32)
    for r in range(N_PIECES):
        for j in range(HEAD_DIRS):
            expand[Q_E + SUBLANES * r + j, j * HEAD_DIM:(j + 1) * HEAD_DIM] = 1.0
    return jnp.asarray(expand).astype(BF16)


def _ssd(xsc, bt, cc, zs, dtt, bias_rep, alog_rep, dskip_exp, ssd_norm, bsz, seq):
    n_chunks = seq // CHUNK
    assert n_chunks % 2 == 0
    expand = _expand_matrix()
    row_spec = lambda width: pl.BlockSpec((None, seq, width), lambda b, g: (b, 0, g))
    par_spec = lambda rows, width: pl.BlockSpec((rows, width), lambda b, g: (0, g))
    return pl.pallas_call(
        functools.partial(_ssd_kernel, seq=seq),
        grid=(bsz, SSD_GROUPS),
        in_specs=[
            row_spec(GROUP_CH),
            pl.BlockSpec((None, None, seq, D_STATE), lambda b, g: (g, b, 0, 0)),
            row_spec(D_STATE), row_spec(GROUP_CH),
            pl.BlockSpec((HEAD_DIRS, seq), lambda b, g: (g, b)),
            pl.BlockSpec((HEAD_DIRS, LANES), lambda b, g: (g, 0)),
            pl.BlockSpec((HEAD_DIRS, LANES), lambda b, g: (g, 0)),
            par_spec(1, GROUP_CH), par_spec(1, GROUP_CH),
            pl.BlockSpec(expand.shape, lambda b, g: (0, 0)),
        ],
        out_specs=pl.BlockSpec((None, seq, GROUP_CH), lambda b, g: (b, 0, g)),
        out_shape=jax.ShapeDtypeStruct((bsz, seq, D_INNER), BF16),
        scratch_shapes=[
            pltpu.VMEM((seq, GROUP_CH), F32),
            pltpu.VMEM((n_chunks, D_STATE, GROUP_CH), F32),
            pltpu.VMEM((n_chunks, D_STATE, GROUP_CH), F32),
            pltpu.VMEM((n_chunks, D_STATE, GROUP_CH), BF16),
            pltpu.VMEM((n_chunks, D_STATE, GROUP_CH), BF16),
            pltpu.VMEM((n_chunks, CHUNK, CHUNK), BF16),
            pltpu.VMEM((2, HEAD_DIRS, seq), F32),
            pltpu.VMEM((n_chunks, GROUP_CH), F32),
            pltpu.VMEM((n_chunks, GROUP_CH), F32),
            pltpu.VMEM((D_STATE, GROUP_CH), F32),
            pltpu.VMEM((D_STATE, GROUP_CH), F32),
        ],
        compiler_params=pltpu.CompilerParams(
            dimension_semantics=("arbitrary", "arbitrary"), vmem_limit_bytes=VMEM_LIMIT),
        name="ssd",
    )(xsc.reshape(bsz, seq, D_INNER), bt.reshape(SSD_GROUPS, bsz, seq, D_STATE),
      cc.reshape(bsz, seq, SSD_GROUPS * D_STATE), zs.reshape(bsz, seq, D_INNER), dtt,
      bias_rep, alog_rep, dskip_exp, ssd_norm, expand)


def _dft_kernel(zr_ref, zi_ref, a1_ref, a2_ref, a3_ref, o_ref, sr_ref, si_ref, *, n2):
    cols = zr_ref.shape[-1]
    rows1 = DFT_N1 * BF16_ROWS
    rows2 = n2 * BF16_ROWS

    def stack(r_ref, i_ref, idx, rows):
        return jnp.concatenate([r_ref[idx].reshape(rows, cols), i_ref[idx].reshape(rows, cols)], axis=0)

    for t2 in range(n2):
        y = _dot(a1_ref[...], stack(zr_ref, zi_ref, (slice(None), t2), rows1))
        sr_ref[:, t2] = y[:rows1].astype(BF16).reshape(DFT_N1, BF16_ROWS, cols)
        si_ref[:, t2] = y[rows1:].astype(BF16).reshape(DFT_N1, BF16_ROWS, cols)
    for t3 in range(BF16_ROWS):
        y = _dot(a2_ref[...], stack(sr_ref, si_ref, t3, rows2))
        sr_ref[t3] = y[:rows2].astype(BF16).reshape(n2, BF16_ROWS, cols)
        si_ref[t3] = y[rows2:].astype(BF16).reshape(n2, BF16_ROWS, cols)
    for k2 in range(n2):
        y = _dot(a3_ref[k2], stack(sr_ref, si_ref, (slice(None), k2), rows1))
        o_ref[:, k2] = y.astype(BF16).reshape(BF16_ROWS, BF16_ROWS, cols)


def _dft(zr, zi, consts, bsz, seq):
    a1, a2, a3 = consts
    n2 = seq // (DFT_N1 * BF16_ROWS)
    shape5 = (bsz, DFT_N1, n2, BF16_ROWS, D_FOURIER)
    spec = pl.BlockSpec((None, DFT_N1, n2, BF16_ROWS, DFT_COLS), lambda b, j: (b, 0, 0, 0, j))
    whole = lambda a: pl.BlockSpec(a.shape, lambda b, j: (0,) * a.ndim)
    out = pl.pallas_call(
        functools.partial(_dft_kernel, n2=n2),
        grid=(bsz, D_FOURIER // DFT_COLS),
        in_specs=[spec, spec, whole(a1), whole(a2), whole(a3)],
        out_specs=spec,
        out_shape=jax.ShapeDtypeStruct(shape5, BF16),
        scratch_shapes=[pltpu.VMEM((BF16_ROWS, n2, BF16_ROWS, DFT_COLS), BF16),
                        pltpu.VMEM((BF16_ROWS, n2, BF16_ROWS, DFT_COLS), BF16)],
        compiler_params=pltpu.CompilerParams(
            dimension_semantics=("arbitrary", "arbitrary"), vmem_limit_bytes=VMEM_LIMIT),
        name="dft",
    )(zr.reshape(shape5), zi.reshape(shape5), a1, a2, a3)
    return out.reshape(bsz * seq, D_FOURIER)


def _channel_dft_matrix():
    cidx = np.arange(FOURIER_GROUP_DIM)
    ang = 2.0 * np.pi * np.outer(cidx, cidx) / FOURIER_GROUP_DIM
    cs = np.concatenate([np.cos(ang), -np.sin(ang)], axis=1)
    return jnp.asarray(cs.astype(np.float32)).astype(BF16)


def _dft_constants(seq):
    n1, n3 = DFT_N1, BF16_ROWS
    n2 = seq // (n1 * n3)
    w = lambda n, e: np.exp(-2j * np.pi * (np.asarray(e) % n) / n)
    i16 = np.arange(n3)
    eye = np.eye(n3)
    m1 = np.einsum("kt,ab->aktb", w(n1, np.outer(np.arange(n1), np.arange(n1))), eye).reshape(n3 * n1, n1 * n3)
    k2 = np.arange(n2)
    f2 = w(n2, np.outer(k2, k2))[:, :, None] * w(n1 * n2, np.outer(k2, np.arange(n1)))[None, :, :]
    m2 = np.einsum("ktc,cd->kctd", f2, np.eye(n1)).reshape(n2 * n1, n2 * n1)
    f3 = (w(n3, np.outer(i16, i16))[None, :, :, None]
          * w(n2 * n3, np.outer(k2, i16))[:, None, :, None]
          * w(seq, np.outer(i16, np.arange(n1)))[None, None, :, :])
    m3 = np.einsum("jktc,cd->jkctd", f3, np.eye(n1)).reshape(n2, n3 * n1, n3 * n1)
    m3 = m3 / math.sqrt(seq * FOURIER_GROUP_DIM)
    full = lambda m: np.concatenate([np.concatenate([m.real, -m.imag], axis=-1),
                                     np.concatenate([m.imag, m.real], axis=-1)], axis=-2)
    real_part = lambda m: np.concatenate([m.real, -m.imag], axis=-1)
    to_bf16 = lambda a: jnp.asarray(a.astype(np.float32)).astype(BF16)
    return to_bf16(full(m1)), to_bf16(full(m2)), to_bf16(real_part(m3))


def _merge_kernel(y_ref, mx_ref, gs_ref, x_ref, wssd_ref, wf_ref, bf_ref, wo_ref, o_ref):
    a_out = _dot(y_ref[...], wssd_ref[...])
    f_out = _dot(mx_ref[...], wf_ref[...]) + bf_ref[...]
    gates = gs_ref[...].astype(F32)
    merged = (gates[:, :D_MODEL] * a_out + gates[:, D_MODEL:] * f_out).astype(BF16)
    o_ref[...] = x_ref[...] + _dot(merged, wo_ref[...])


def _merge(y2d, mixed, gs, x2d, w_ssd, w_f, b_f, w_o, bm):
    t = x2d.shape[0]
    const = lambda r, c: pl.BlockSpec((r, c), lambda i: (0, 0))
    return pl.pallas_call(
        _merge_kernel,
        grid=(t // bm,),
        in_specs=[
            pl.BlockSpec((bm, D_INNER), lambda i: (i, 0)),
            pl.BlockSpec((bm, D_FOURIER), lambda i: (i, 0)),
            pl.BlockSpec((bm, 2 * D_MODEL), lambda i: (i, 0)),
            pl.BlockSpec((bm, D_MODEL), lambda i: (i, 0)),
            const(D_INNER, D_MODEL), const(D_FOURIER, D_MODEL), const(1, D_MODEL), const(D_MODEL, D_MODEL),
        ],
        out_specs=pl.BlockSpec((bm, D_MODEL), lambda i: (i, 0)),
        out_shape=jax.ShapeDtypeStruct((t, D_MODEL), F32),
        compiler_params=pltpu.CompilerParams(
            dimension_semantics=("arbitrary",), vmem_limit_bytes=VMEM_LIMIT),
        name="merge",
    )(y2d, mixed, gs, x2d, w_ssd, w_f, b_f, w_o)


def _ffn_kernel(x_ref, g_ref, wg_ref, wu_ref, wd_ref, gfin_ref, o_ref, h_ref, acc_ref, *, n_ff):
    f = pl.program_id(1)

    @pl.when(f == 0)
    def _():
        h_ref[...] = (_rms_scale(x_ref[...]) * g_ref[...]).astype(BF16)
        acc_ref[...] = jnp.zeros_like(acc_ref)

    h = h_ref[...]
    gate = _dot(h, wg_ref[...])
    up = _dot(h, wu_ref[...])
    act = (gate * _sigmoid(gate) * up).astype(BF16)
    acc_ref[...] += _dot(act, wd_ref[...])

    @pl.when(f == n_ff - 1)
    def _():
        o_ref[...] = _rms_scale(x_ref[...] + acc_ref[...]) * gfin_ref[...]


def _ffn(x1, norm_g, w_gate, w_up, w_down, norm_fin, bm, ff_tile):
    t = x1.shape[0]
    n_ff = D_FF // ff_tile
    return pl.pallas_call(
        functools.partial(_ffn_kernel, n_ff=n_ff),
        grid=(t // bm, n_ff),
        in_specs=[
            pl.BlockSpec((bm, D_MODEL), lambda i, f: (i, 0)),
            pl.BlockSpec((1, D_MODEL), lambda i, f: (0, 0)),
            pl.BlockSpec((D_MODEL, ff_tile), lambda i, f: (0, f)),
            pl.BlockSpec((D_MODEL, ff_tile), lambda i, f: (0, f)),
            pl.BlockSpec((ff_tile, D_MODEL), lambda i, f: (f, 0)),
            pl.BlockSpec((1, D_MODEL), lambda i, f: (0, 0)),
        ],
        out_specs=pl.BlockSpec((bm, D_MODEL), lambda i, f: (i, 0)),
        out_shape=jax.ShapeDtypeStruct((t, D_MODEL), F32),
        scratch_shapes=[pltpu.VMEM((bm, D_MODEL), BF16), pltpu.VMEM((bm, D_MODEL), F32)],
        compiler_params=pltpu.CompilerParams(
            dimension_semantics=("arbitrary", "arbitrary"), vmem_limit_bytes=VMEM_LIMIT),
        name="ffn",
    )(x1, norm_g, w_gate, w_up, w_down, norm_fin)


def _prep_weights(norm_mix, w_in, conv_w, conv_b, dt_bias_f, dt_bias_b, a_log_f, a_log_b, d_skip,
                  ssd_norm, w_ssd_out, w_fourier_out, b_fourier_out, w_out, norm_ffn, w_gate_up,
                  w_down, norm_final):
    o_dt = D_INNER + (D_INNER + 2 * SSD_GROUPS * D_STATE)
    o_u = o_dt + 2 * SSD_HEADS
    o_g = o_u + D_FOURIER
    w_all = jnp.concatenate([w_in[:, :o_dt], w_in[:, o_g:], w_in[:, o_u:o_g]], axis=1).astype(BF16)

    def by_group(f, b):
        return jnp.concatenate([f.reshape(SSD_GROUPS, HEADS_PER_GROUP),
                                b.reshape(SSD_GROUPS, HEADS_PER_GROUP)], axis=1).reshape(-1)

    w_dt_cols = w_in[:, o_dt:o_u]
    w_dt = jnp.concatenate([w_dt_cols[:, :SSD_HEADS].reshape(D_MODEL, SSD_GROUPS, HEADS_PER_GROUP),
                            w_dt_cols[:, SSD_HEADS:].reshape(D_MODEL, SSD_GROUPS, HEADS_PER_GROUP)],
                           axis=2).reshape(D_MODEL, 2 * SSD_HEADS)
    w_dt = jnp.pad(w_dt, ((0, 0), (0, LANES - 2 * SSD_HEADS))).astype(BF16)
    rep = lambda v: jnp.broadcast_to(v.astype(F32)[:, None], (2 * SSD_HEADS, LANES))
    return dict(
        norm_mix=norm_mix.reshape(1, D_MODEL), w_all=w_all, w_dt=w_dt,
        conv_w=conv_w, conv_b=conv_b.reshape(1, -1),
        bias_rep=rep(by_group(dt_bias_f, dt_bias_b)), alog_rep=rep(by_group(a_log_f, a_log_b)),
        dskip_exp=jnp.repeat(d_skip.astype(F32), HEAD_DIM).reshape(1, D_INNER),
        ssd_norm=ssd_norm.reshape(1, D_INNER),
        w_ssd=w_ssd_out.astype(BF16), w_f=w_fourier_out.astype(BF16),
        b_f=b_fourier_out.reshape(1, D_MODEL), w_o=w_out.astype(BF16),
        norm_ffn=norm_ffn.reshape(1, D_MODEL),
        w_gate=w_gate_up[:, :D_FF].astype(BF16), w_up=w_gate_up[:, D_FF:].astype(BF16),
        w_down=w_down.astype(BF16), norm_final=norm_final.reshape(1, D_MODEL),
    )


def _trunk(x, p):
    bsz, seq, _ = x.shape
    t = bsz * seq
    bm = min(1024, t)
    x2d = x.reshape(t, D_MODEL)
    zs, xsc, bt, cc, gs, zr, zi, dtt = _inproj(x2d, p["norm_mix"], p["w_all"], p["w_dt"],
                                               _channel_dft_matrix(), p["conv_w"], p["conv_b"], bm, seq)
    y = _ssd(xsc, bt, cc, zs, dtt, p["bias_rep"], p["alog_rep"], p["dskip_exp"], p["ssd_norm"], bsz, seq)
    mixed = _dft(zr, zi, _dft_constants(seq), bsz, seq)
    x1 = _merge(y.reshape(t, D_INNER), mixed, gs, x2d, p["w_ssd"], p["w_f"], p["b_f"], p["w_o"],
                min(512, t))
    out = _ffn(x1, p["norm_ffn"], p["w_gate"], p["w_up"], p["w_down"], p["norm_final"], bm, D_FF // 2)
    return out.reshape(bsz, seq, D_MODEL)


def kernel(x_prompt, x_sample, norm_mix, w_in, conv_w, conv_b, dt_bias_f, dt_bias_b, a_log_f, a_log_b,
           d_skip, ssd_norm, w_ssd_out, w_fourier_out, b_fourier_out, w_out, norm_ffn, w_gate_up, w_down,
           norm_final):
    p = _prep_weights(norm_mix[0], w_in[0], conv_w[0], conv_b[0], dt_bias_f[0], dt_bias_b[0],
                      a_log_f[0], a_log_b[0], d_skip[0], ssd_norm[0], w_ssd_out[0], w_fourier_out[0],
                      b_fourier_out[0], w_out[0], norm_ffn[0], w_gate_up[0], w_down[0], norm_final)
    return (_trunk(x_prompt, p), _trunk(x_sample, p))
```

```python
import functools
import math

import numpy as np
import jax
import jax.numpy as jnp
from jax import lax
from jax.experimental import pallas as pl
from jax.experimental.pallas import tpu as pltpu

F32 = jnp.float32
BF16 = jnp.bfloat16

D_MODEL = 1024
D_INNER = 2048
HEAD_DIM = 64
SSD_HEADS = 32
SSD_GROUPS = 8
HEADS_PER_GROUP = 4
D_STATE = 128
CONV_WIDTH = 7
CONV_PAD = CONV_WIDTH // 2
CHUNK = 128
GROUP_CH = D_INNER // SSD_GROUPS
D_FOURIER = 1024
FOURIER_GROUP_DIM = 128
FOURIER_GROUPS = 8
D_FF = 2816
EPS = 1e-5

LANES = 128
SUBLANES = 8
BF16_ROWS = 16
DFT_N1 = 16
DFT_COLS = 256
PROJ_TILE = 1024
N_MAIN = 8192
VMEM_LIMIT = 56 * 1024 * 1024


def _silu_of_twice(t):
    return t + t * jnp.tanh(t)


def _sigmoid_of_twice(t):
    return 0.5 + 0.5 * jnp.tanh(t)


def _softplus(v):
    return jnp.maximum(v, 0.0) + jnp.log1p(jnp.exp(-jnp.abs(v)))


def _rms_scale(v):
    return v * lax.rsqrt(jnp.mean(v * v, axis=-1, keepdims=True) + EPS)


def _dot(a, b):
    return jnp.dot(a, b, preferred_element_type=F32)


TILE_XS = 2
TILE_B = 4
TILE_C = 5
TILE_GATE = 6
TILE_FOURIER = 8
HALO = BF16_ROWS


def _inproj_kernel(x_ref, xp_ref, xn_ref, g_ref, w_ref, wdt_ref, cs_ref, cw_ref, cb_ref,
                   zs_ref, xsc_ref, bt_ref, cc_ref, gs_ref, zr_ref, zi_ref, dtt_ref,
                   h_ref, slab_ref, conv_ref, *, tiles_per_seq):
    i = pl.program_id(0)
    j = pl.program_id(1)
    bm = x_ref.shape[0]
    main = pl.ds(HALO, bm)

    @pl.when(j == 0)
    def _():
        g = g_ref[...]
        h = (_rms_scale(x_ref[...]) * g).astype(BF16)
        h_ref[main, :] = h
        pos = i % tiles_per_seq
        hp = _rms_scale(xp_ref[0]) * g
        hn = _rms_scale(xn_ref[0]) * g
        h_ref[0:HALO, :] = jnp.where(pos == 0, 0.0, hp).astype(BF16)
        h_ref[HALO + bm:, :] = jnp.where(pos == tiles_per_seq - 1, 0.0, hn).astype(BF16)
        dtt_ref[...] = _dot(h, wdt_ref[...]).T

    @pl.when(j < TILE_XS)
    def _():
        zs_ref[...] = _silu_of_twice(_dot(h_ref[main, :], w_ref[...])).astype(BF16)

    @pl.when((j >= TILE_XS) & (j < TILE_GATE))
    def _():
        for q in range(PROJ_TILE // (2 * LANES)):
            pe = _dot(h_ref[...], w_ref[:, q * 2 * LANES:(q + 1) * 2 * LANES])
            for s in range(2 * q, 2 * q + 2):
                sl = slice(s * LANES, (s + 1) * LANES)
                slab_ref[s] = pe[:, (s - 2 * q) * LANES:(s - 2 * q + 1) * LANES]
                acc = cb_ref[:, sl]
                for k in range(CONV_WIDTH):
                    acc = acc + cw_ref[k:k + 1, sl] * slab_ref[s, HALO - CONV_PAD + k:HALO - CONV_PAD + k + bm, :]
                conv_ref[:, sl] = _silu_of_twice(acc).astype(BF16)

    @pl.when((j >= TILE_XS) & (j < TILE_B))
    def _():
        xsc_ref[...] = conv_ref[...]

    @pl.when(j == TILE_B)
    def _():
        for g in range(SSD_GROUPS):
            for c in range(bm // CHUNK):
                rows = slice(c * CHUNK, (c + 1) * CHUNK)
                blk = conv_ref[rows, g * D_STATE:(g + 1) * D_STATE].astype(F32)
                bt_ref[g, rows, :] = blk.T.astype(BF16)

    @pl.when(j == TILE_C)
    def _():
        cc_ref[...] = conv_ref[...]

    @pl.when((j >= TILE_GATE) & (j < TILE_FOURIER))
    def _():
        gs_ref[...] = _sigmoid_of_twice(_dot(h_ref[main, :], w_ref[...])).astype(BF16)

    @pl.when(j == TILE_FOURIER)
    def _():
        u = _dot(h_ref[main, :], w_ref[...]).astype(BF16)
        for g in range(FOURIER_GROUPS):
            sl = slice(g * FOURIER_GROUP_DIM, (g + 1) * FOURIER_GROUP_DIM)
            z = _dot(u[:, sl], cs_ref[...])
            zr_ref[:, sl] = z[:, :FOURIER_GROUP_DIM].astype(BF16)
            zi_ref[:, sl] = z[:, FOURIER_GROUP_DIM:].astype(BF16)


def _inproj(x2d, norm_g, w_all, w_dt, cs, conv_w, conv_b, bm, seq):
    t = x2d.shape[0]
    n_tiles = w_all.shape[1] // PROJ_TILE
    assert n_tiles == TILE_FOURIER + 1 and seq % bm == 0 and bm % HALO == 0
    halo_blocks = bm // HALO
    x_halo = x2d.reshape(t // HALO, HALO, D_MODEL)
    clip = lambda v, lo, hi: jnp.minimum(jnp.maximum(v, lo), hi)
    tok = lambda width, first, count: pl.BlockSpec(
        (bm, width), lambda i, j: (i, clip(j - first, 0, count - 1)))
    bf16_out = lambda cols: jax.ShapeDtypeStruct((t, cols), BF16)
    return pl.pallas_call(
        functools.partial(_inproj_kernel, tiles_per_seq=seq // bm),
        grid=(t // bm, n_tiles),
        in_specs=[
            pl.BlockSpec((bm, D_MODEL), lambda i, j: (i, 0)),
            pl.BlockSpec((1, HALO, D_MODEL), lambda i, j: (jnp.maximum(i * halo_blocks - 1, 0), 0, 0)),
            pl.BlockSpec((1, HALO, D_MODEL),
                         lambda i, j: (jnp.minimum((i + 1) * halo_blocks, t // HALO - 1), 0, 0)),
            pl.BlockSpec((1, D_MODEL), lambda i, j: (0, 0)),
            pl.BlockSpec((D_MODEL, PROJ_TILE), lambda i, j: (0, j)),
            pl.BlockSpec((D_MODEL, LANES), lambda i, j: (0, 0)),
            pl.BlockSpec((FOURIER_GROUP_DIM, 2 * FOURIER_GROUP_DIM), lambda i, j: (0, 0)),
            pl.BlockSpec((CONV_WIDTH, PROJ_TILE), lambda i, j: (0, clip(j - TILE_XS, 0, 3))),
            pl.BlockSpec((1, PROJ_TILE), lambda i, j: (0, clip(j - TILE_XS, 0, 3))),
        ],
        out_specs=[
            tok(PROJ_TILE, 0, 2),
            tok(PROJ_TILE, TILE_XS, 2),
            pl.BlockSpec((SSD_GROUPS, bm, D_STATE), lambda i, j: (0, i, 0)),
            tok(PROJ_TILE, TILE_C, 1),
            tok(PROJ_TILE, TILE_GATE, 2),
            tok(D_FOURIER, TILE_FOURIER, 1),
            tok(D_FOURIER, TILE_FOURIER, 1),
            pl.BlockSpec((LANES, bm), lambda i, j: (0, i)),
        ],
        out_shape=[
            bf16_out(D_INNER), bf16_out(D_INNER),
            jax.ShapeDtypeStruct((SSD_GROUPS, t, D_STATE), BF16),
            bf16_out(SSD_GROUPS * D_STATE), bf16_out(2 * D_MODEL),
            bf16_out(D_FOURIER), bf16_out(D_FOURIER),
            jax.ShapeDtypeStruct((LANES, t), F32),
        ],
        scratch_shapes=[
            pltpu.VMEM((bm + 2 * HALO, D_MODEL), BF16),
            pltpu.VMEM((PROJ_TILE // LANES, bm + 2 * HALO, LANES), F32),
            pltpu.VMEM((bm, PROJ_TILE), BF16),
        ],
        compiler_params=pltpu.CompilerParams(
            dimension_semantics=("arbitrary", "arbitrary"), vmem_limit_bytes=VMEM_LIMIT),
        name="inproj",
    )(x2d, x_halo, x_halo, norm_g, w_all, w_dt, cs, conv_w, conv_b)


Q_E = 48
N_PIECES = 3
HEAD_DIRS = 2 * HEADS_PER_GROUP
SSD_UNROLL = 8


def _split3(v):
    p0 = v.astype(BF16).astype(F32)
    r1 = v - p0
    p1 = r1.astype(BF16).astype(F32)
    return p0, p1, (r1 - p1).astype(BF16).astype(F32)


def _ssd_kernel(xsb_ref, bt_ref, cc_ref, zs_ref, dtt_ref,
                bias_ref, alog_ref, dsk_ref, nrm_ref, exp_ref,
                out_ref,
                y_ref, locf_ref, locb_ref, sinf_ref, sinb_ref,
                qt_ref, rows_ref, cdf_ref, cdb_ref, stf_ref, stb_ref, *, seq):
    n_chunks = seq // CHUNK

    def per_chunk_group(fn, width):
        def body(i, carry):
            for u in range(width):
                fn(width * i + u)
            return carry
        lax.fori_loop(0, n_chunks // width, body, 0)

    row_s = lax.broadcasted_iota(jnp.int32, (HEAD_DIRS, seq), 0)
    lane_s = lax.broadcasted_iota(jnp.int32, (HEAD_DIRS, seq), 1) & (CHUNK - 1)
    dt_all = _softplus(dtt_ref[...] + bias_ref[:, 0:1])
    dta = dt_all * (-jnp.exp(alog_ref[:, 0:1]))
    cf = dta
    rb = dta
    k = 1
    while k < CHUNK:
        cf = cf + jnp.where(lane_s >= k, pltpu.roll(cf, k, 1), 0.0)
        rb = rb + jnp.where(lane_s < CHUNK - k, pltpu.roll(rb, seq - k, 1), 0.0)
        k *= 2
    rows_ref[0] = dt_all
    rows_ref[1] = jnp.where(row_s < HEADS_PER_GROUP, cf, rb)

    dskip = dsk_ref[...]

    row8 = lax.broadcasted_iota(jnp.int32, (SUBLANES, LANES), 0)
    li = lax.broadcasted_iota(jnp.int32, (CHUNK, CHUNK), 0)
    si = lax.broadcasted_iota(jnp.int32, (CHUNK, CHUNK), 1)
    lane_g = lax.broadcasted_iota(jnp.int32, (CHUNK, GROUP_CH), 1)
    is_fwd_row = row8 < HEADS_PER_GROUP
    zero8 = jnp.zeros((SUBLANES, LANES), F32)
    lower_b = jnp.where(li >= si, 1.0, 0.0).astype(BF16)
    upper_b = jnp.where(li <= si, 1.0, 0.0).astype(BF16)

    def diag_pass(c):
        t0 = pl.multiple_of(c * CHUNK, CHUNK)
        cb = cc_ref[pl.ds(t0, CHUNK), :]
        bt = bt_ref[pl.ds(t0, CHUNK), :]
        xb = xsb_ref[pl.ds(t0, CHUNK), :]
        scores = _dot(cb, bt).astype(BF16)

        dt = rows_ref[0, :, pl.ds(t0, CHUNK)]
        cum = rows_ref[1, :, pl.ds(t0, CHUNK)]
        tot = jnp.where(is_fwd_row, cum[:, CHUNK - 1:CHUNK], cum[:, 0:1])
        wrow = dt * jnp.exp(tot - cum)
        ep = _split3(jnp.exp(cum))

        q = jnp.concatenate([cum] + [zero8] * (Q_E // SUBLANES - 1) + [ep[0], ep[1], ep[2]]
                            + [zero8] * ((CHUNK - Q_E) // SUBLANES - N_PIECES), axis=0)
        qtf = q.T
        qt_ref[c] = qtf.astype(BF16)

        edge = jnp.concatenate([qtf[0:SUBLANES, :], qtf[CHUNK - SUBLANES:, :]], axis=0).astype(BF16)
        cd = _dot(edge, exp_ref[...])
        cdf_ref[pl.ds(c, 1), :] = cd[2 * SUBLANES - 1:2 * SUBLANES, :GROUP_CH]
        cdb_ref[pl.ds(c, 1), :] = cd[0:1, GROUP_CH:]

        dtb = dt.astype(BF16)
        wb = wrow.astype(BF16)
        lhs_m = []
        lhs_f = []
        lhs_b = []
        xm = []
        for h in range(HEADS_PER_GROUP):
            hb = HEADS_PER_GROUP + h
            arg = jnp.where(li >= si, qtf[:, h:h + 1] - cum[h:h + 1, :], qtf[:, hb:hb + 1] - cum[hb:hb + 1, :])
            coef = lower_b * dtb[h:h + 1, :] + upper_b * dtb[hb:hb + 1, :]
            lhs_m.append(scores * jnp.exp(arg).astype(BF16) * coef)
            lhs_f.append(bt * wb[h:h + 1, :])
            lhs_b.append(bt * wb[hb:hb + 1, :])
            in_head = (lane_g >= h * HEAD_DIM) & (lane_g < (h + 1) * HEAD_DIM)
            xm.append(jnp.where(in_head, xb, jnp.zeros_like(xb)))
        lhs = jnp.concatenate([jnp.concatenate(lhs_m, axis=1),
                               jnp.concatenate(lhs_f, axis=1),
                               jnp.concatenate(lhs_b, axis=1)], axis=0)
        big = _dot(lhs, jnp.concatenate(xm, axis=0))
        y_ref[pl.ds(t0, CHUNK), :] = big[:CHUNK] + dskip * xb.astype(F32)
        locf_ref[c] = big[CHUNK:2 * CHUNK]
        locb_ref[c] = big[2 * CHUNK:]

    per_chunk_group(diag_pass, SSD_UNROLL)

    stf_ref[...] = jnp.zeros((D_STATE, GROUP_CH), F32)
    stb_ref[...] = jnp.zeros((D_STATE, GROUP_CH), F32)

    def scan(i, carry):
        j = n_chunks - 1 - i
        sf = stf_ref[...]
        sinf_ref[i] = sf.astype(BF16)
        stf_ref[...] = sf * cdf_ref[pl.ds(i, 1), :] + locf_ref[i]
        sb = stb_ref[...]
        sinb_ref[j] = sb.astype(BF16)
        stb_ref[...] = sb * cdb_ref[pl.ds(j, 1), :] + locb_ref[j]
        return carry

    lax.fori_loop(0, n_chunks, scan, 0)

    nrm = nrm_ref[...]

    def final_pass(c):
        t0 = pl.multiple_of(c * CHUNK, CHUNK)
        st = jnp.concatenate([sinf_ref[c], sinb_ref[c]], axis=1)
        yo = _dot(cc_ref[pl.ds(t0, CHUNK), :], st)
        ee = _dot(qt_ref[c], exp_ref[...])
        y = (y_ref[pl.ds(t0, CHUNK), :] + yo[:, :GROUP_CH] * ee[:, :GROUP_CH]
             + yo[:, GROUP_CH:] * ee[:, GROUP_CH:])
        y = y * zs_ref[pl.ds(t0, CHUNK), :].astype(F32)
        out_ref[pl.ds(t0, CHUNK), :] = (_rms_scale(y) * nrm).astype(BF16)

    per_chunk_group(final_pass, 2 * SSD_UNROLL)


def _expand_matrix():
    expand = np.zeros((CHUNK, 2 * GROUP_CH), np.float32)
    for r in range(N_PIECES):
        for j in range(HEAD_DIRS):
            expand[Q_E + SUBLANES * r + j, j * HEAD_DIM:(j + 1) * HEAD_DIM] = 1.0
    return jnp.asarray(expand).astype(BF16)


def _ssd(xsc, bt, cc, zs, dtt, bias_rep, alog_rep, dskip_exp, ssd_norm, bsz, seq):
    n_chunks = seq // CHUNK
    assert n_chunks % 2 == 0
    expand = _expand_matrix()
    row_spec = lambda width: pl.BlockSpec((None, seq, width), lambda b, g: (b, 0, g))
    par_spec = lambda rows, width: pl.BlockSpec((rows, width), lambda b, g: (0, g))
    return pl.pallas_call(
        functools.partial(_ssd_kernel, seq=seq),
        grid=(bsz, SSD_GROUPS),
        in_specs=[
            row_spec(GROUP_CH),
            pl.BlockSpec((None, None, seq, D_STATE), lambda b, g: (g, b, 0, 0)),
            row_spec(D_STATE), row_spec(GROUP_CH),
            pl.BlockSpec((HEAD_DIRS, seq), lambda b, g: (g, b)),
            pl.BlockSpec((HEAD_DIRS, LANES), lambda b, g: (g, 0)),
            pl.BlockSpec((HEAD_DIRS, LANES), lambda b, g: (g, 0)),
            par_spec(1, GROUP_CH), par_spec(1, GROUP_CH),
            pl.BlockSpec(expand.shape, lambda b, g: (0, 0)),
        ],
        out_specs=pl.BlockSpec((None, seq, GROUP_CH), lambda b, g: (b, 0, g)),
        out_shape=jax.ShapeDtypeStruct((bsz, seq, D_INNER), BF16),
        scratch_shapes=[
            pltpu.VMEM((seq, GROUP_CH), F32),
            pltpu.VMEM((n_chunks, D_STATE, GROUP_CH), F32),
            pltpu.VMEM((n_chunks, D_STATE, GROUP_CH), F32),
            pltpu.VMEM((n_chunks, D_STATE, GROUP_CH), BF16),
            pltpu.VMEM((n_chunks, D_STATE, GROUP_CH), BF16),
            pltpu.VMEM((n_chunks, CHUNK, CHUNK), BF16),
            pltpu.VMEM((2, HEAD_DIRS, seq), F32),
            pltpu.VMEM((n_chunks, GROUP_CH), F32),
            pltpu.VMEM((n_chunks, GROUP_CH), F32),
            pltpu.VMEM((D_STATE, GROUP_CH), F32),
            pltpu.VMEM((D_STATE, GROUP_CH), F32),
        ],
        compiler_params=pltpu.CompilerParams(
            dimension_semantics=("arbitrary", "arbitrary"), vmem_limit_bytes=VMEM_LIMIT),
        name="ssd",
    )(xsc.reshape(bsz, seq, D_INNER), bt.reshape(SSD_GROUPS, bsz, seq, D_STATE),
      cc.reshape(bsz, seq, SSD_GROUPS * D_STATE), zs.reshape(bsz, seq, D_INNER), dtt,
      bias_rep, alog_rep, dskip_exp, ssd_norm, expand)


def _dft_kernel(zr_ref, zi_ref, a1_ref, a2_ref, a3_ref, o_ref, sr_ref, si_ref, *, n2):
    cols = zr_ref.shape[-1]
    rows1 = DFT_N1 * BF16_ROWS
    rows2 = n2 * BF16_ROWS

    def stack(r_ref, i_ref, idx, rows):
        return jnp.concatenate([r_ref[idx].reshape(rows, cols), i_ref[idx].reshape(rows, cols)], axis=0)

    for t2 in range(n2):
        y = _dot(a1_ref[...], stack(zr_ref, zi_ref, (slice(None), t2), rows1))
        sr_ref[:, t2] = y[:rows1].astype(BF16).reshape(DFT_N1, BF16_ROWS, cols)
        si_ref[:, t2] = y[rows1:].astype(BF16).reshape(DFT_N1, BF16_ROWS, cols)
    for t3 in range(BF16_ROWS):
        y = _dot(a2_ref[...], stack(sr_ref, si_ref, t3, rows2))
        sr_ref[t3] = y[:rows2].astype(BF16).reshape(n2, BF16_ROWS, cols)
        si_ref[t3] = y[rows2:].astype(BF16).reshape(n2, BF16_ROWS, cols)
    for k2 in range(n2):
        y = _dot(a3_ref[k2], stack(sr_ref, si_ref, (slice(None), k2), rows1))
        o_ref[:, k2] = y.astype(BF16).reshape(BF16_ROWS, BF16_ROWS, cols)


def _dft(zr, zi, consts, bsz, seq):
    a1, a2, a3 = consts
    n2 = seq // (DFT_N1 * BF16_ROWS)
    shape5 = (bsz, DFT_N1, n2, BF16_ROWS, D_FOURIER)
    spec = pl.BlockSpec((None, DFT_N1, n2, BF16_ROWS, DFT_COLS), lambda b, j: (b, 0, 0, 0, j))
    whole = lambda a: pl.BlockSpec(a.shape, lambda b, j: (0,) * a.ndim)
    out = pl.pallas_call(
        functools.partial(_dft_kernel, n2=n2),
        grid=(bsz, D_FOURIER // DFT_COLS),
        in_specs=[spec, spec, whole(a1), whole(a2), whole(a3)],
        out_specs=spec,
        out_shape=jax.ShapeDtypeStruct(shape5, BF16),
        scratch_shapes=[pltpu.VMEM((BF16_ROWS, n2, BF16_ROWS, DFT_COLS), BF16),
                        pltpu.VMEM((BF16_ROWS, n2, BF16_ROWS, DFT_COLS), BF16)],
        compiler_params=pltpu.CompilerParams(
            dimension_semantics=("arbitrary", "arbitrary"), vmem_limit_bytes=VMEM_LIMIT),
        name="dft",
    )(zr.reshape(shape5), zi.reshape(shape5), a1, a2, a3)
    return out.reshape(bsz * seq, D_FOURIER)


def _channel_dft_matrix():
    cidx = np.arange(FOURIER_GROUP_DIM)
    ang = 2.0 * np.pi * np.outer(cidx, cidx) / FOURIER_GROUP_DIM
    cs = np.concatenate([np.cos(ang), -np.sin(ang)], axis=1)
    return jnp.asarray(cs.astype(np.float32)).astype(BF16)


def _dft_constants(seq):
    n1, n3 = DFT_N1, BF16_ROWS
    n2 = seq // (n1 * n3)
    w = lambda n, e: np.exp(-2j * np.pi * (np.asarray(e) % n) / n)
    i16 = np.arange(n3)
    eye = np.eye(n3)
    m1 = np.einsum("kt,ab->aktb", w(n1, np.outer(np.arange(n1), np.arange(n1))), eye).reshape(n3 * n1, n1 * n3)
    k2 = np.arange(n2)
    f2 = w(n2, np.outer(k2, k2))[:, :, None] * w(n1 * n2, np.outer(k2, np.arange(n1)))[None, :, :]
    m2 = np.einsum("ktc,cd->kctd", f2, np.eye(n1)).reshape(n2 * n1, n2 * n1)
    f3 = (w(n3, np.outer(i16, i16))[None, :, :, None]
          * w(n2 * n3, np.outer(k2, i16))[:, None, :, None]
          * w(seq, np.outer(i16, np.arange(n1)))[None, None, :, :])
    m3 = np.einsum("jktc,cd->jkctd", f3, np.eye(n1)).reshape(n2, n3 * n1, n3 * n1)
    m3 = m3 / math.sqrt(seq * FOURIER_GROUP_DIM)
    full = lambda m: np.concatenate([np.concatenate([m.real, -m.imag], axis=-1),
                                     np.concatenate([m.imag, m.real], axis=-1)], axis=-2)
    real_part = lambda m: np.concatenate([m.real, -m.imag], axis=-1)
    to_bf16 = lambda a: jnp.asarray(a.astype(np.float32)).astype(BF16)
    return to_bf16(full(m1)), to_bf16(full(m2)), to_bf16(real_part(m3))


def _merge_kernel(y_ref, mx_ref, gs_ref, x_ref, wssd_ref, wf_ref, bf_ref, wo_ref, o_ref):
    a_out = _dot(y_ref[...], wssd_ref[...])
    f_out = _dot(mx_ref[...], wf_ref[...]) + bf_ref[...]
    gates = gs_ref[...].astype(F32)
    merged = (gates[:, :D_MODEL] * a_out + gates[:, D_MODEL:] * f_out).astype(BF16)
    o_ref[...] = x_ref[...] + _dot(merged, wo_ref[...])


def _merge(y2d, mixed, gs, x2d, w_ssd, w_f, b_f, w_o, bm):
    t = x2d.shape[0]
    const = lambda r, c: pl.BlockSpec((r, c), lambda i: (0, 0))
    return pl.pallas_call(
        _merge_kernel,
        grid=(t // bm,),
        in_specs=[
            pl.BlockSpec((bm, D_INNER), lambda i: (i, 0)),
            pl.BlockSpec((bm, D_FOURIER), lambda i: (i, 0)),
            pl.BlockSpec((bm, 2 * D_MODEL), lambda i: (i, 0)),
            pl.BlockSpec((bm, D_MODEL), lambda i: (i, 0)),
            const(D_INNER, D_MODEL), const(D_FOURIER, D_MODEL), const(1, D_MODEL), const(D_MODEL, D_MODEL),
        ],
        out_specs=pl.BlockSpec((bm, D_MODEL), lambda i: (i, 0)),
        out_shape=jax.ShapeDtypeStruct((t, D_MODEL), F32),
        compiler_params=pltpu.CompilerParams(
            dimension_semantics=("arbitrary",), vmem_limit_bytes=VMEM_LIMIT),
        name="merge",
    )(y2d, mixed, gs, x2d, w_ssd, w_f, b_f, w_o)


def _ffn_kernel(x_ref, g_ref, wg_ref, wu_ref, wd_ref, gfin_ref, o_ref, h_ref, acc_ref, *, n_ff):
    f = pl.program_id(1)

    @pl.when(f == 0)
    def _():
        h_ref[...] = (_rms_scale(x_ref[...]) * g_ref[...]).astype(BF16)
        acc_ref[...] = jnp.zeros_like(acc_ref)

    h = h_ref[...]
    gate = _dot(h, wg_ref[...])
    up = _dot(h, wu_ref[...])
    act = (_silu_of_twice(gate) * up).astype(BF16)
    acc_ref[...] += _dot(act, wd_ref[...])

    @pl.when(f == n_ff - 1)
    def _():
        o_ref[...] = _rms_scale(x_ref[...] + acc_ref[...]) * gfin_ref[...]


def _ffn(x1, norm_g, w_gate, w_up, w_down, norm_fin, bm, ff_tile):
    t = x1.shape[0]
    n_ff = D_FF // ff_tile
    return pl.pallas_call(
        functools.partial(_ffn_kernel, n_ff=n_ff),
        grid=(t // bm, n_ff),
        in_specs=[
            pl.BlockSpec((bm, D_MODEL), lambda i, f: (i, 0)),
            pl.BlockSpec((1, D_MODEL), lambda i, f: (0, 0)),
            pl.BlockSpec((D_MODEL, ff_tile), lambda i, f: (0, f)),
            pl.BlockSpec((D_MODEL, ff_tile), lambda i, f: (0, f)),
            pl.BlockSpec((ff_tile, D_MODEL), lambda i, f: (f, 0)),
            pl.BlockSpec((1, D_MODEL), lambda i, f: (0, 0)),
        ],
        out_specs=pl.BlockSpec((bm, D_MODEL), lambda i, f: (i, 0)),
        out_shape=jax.ShapeDtypeStruct((t, D_MODEL), F32),
        scratch_shapes=[pltpu.VMEM((bm, D_MODEL), BF16), pltpu.VMEM((bm, D_MODEL), F32)],
        compiler_params=pltpu.CompilerParams(
            dimension_semantics=("arbitrary", "arbitrary"), vmem_limit_bytes=VMEM_LIMIT),
        name="ffn",
    )(x1, norm_g, w_gate, w_up, w_down, norm_fin)


def _prep_weights(norm_mix, w_in, conv_w, conv_b, dt_bias_f, dt_bias_b, a_log_f, a_log_b, d_skip,
                  ssd_norm, w_ssd_out, w_fourier_out, b_fourier_out, w_out, norm_ffn, w_gate_up,
                  w_down, norm_final):
    o_dt = D_INNER + (D_INNER + 2 * SSD_GROUPS * D_STATE)
    o_u = o_dt + 2 * SSD_HEADS
    o_g = o_u + D_FOURIER
    w_all = jnp.concatenate([0.5 * w_in[:, :D_INNER], w_in[:, D_INNER:o_dt], 0.5 * w_in[:, o_g:],
                             w_in[:, o_u:o_g]], axis=1).astype(BF16)

    def by_group(f, b):
        return jnp.concatenate([f.reshape(SSD_GROUPS, HEADS_PER_GROUP),
                                b.reshape(SSD_GROUPS, HEADS_PER_GROUP)], axis=1).reshape(-1)

    w_dt_cols = w_in[:, o_dt:o_u]
    w_dt = jnp.concatenate([w_dt_cols[:, :SSD_HEADS].reshape(D_MODEL, SSD_GROUPS, HEADS_PER_GROUP),
                            w_dt_cols[:, SSD_HEADS:].reshape(D_MODEL, SSD_GROUPS, HEADS_PER_GROUP)],
                           axis=2).reshape(D_MODEL, 2 * SSD_HEADS)
    w_dt = jnp.pad(w_dt, ((0, 0), (0, LANES - 2 * SSD_HEADS))).astype(BF16)
    rep = lambda v: jnp.broadcast_to(v.astype(F32)[:, None], (2 * SSD_HEADS, LANES))
    return dict(
        norm_mix=norm_mix.reshape(1, D_MODEL), w_all=w_all, w_dt=w_dt,
        conv_w=0.5 * conv_w, conv_b=0.5 * conv_b.reshape(1, -1),
        bias_rep=rep(by_group(dt_bias_f, dt_bias_b)), alog_rep=rep(by_group(a_log_f, a_log_b)),
        dskip_exp=jnp.repeat(d_skip.astype(F32), HEAD_DIM).reshape(1, D_INNER),
        ssd_norm=ssd_norm.reshape(1, D_INNER),
        w_ssd=w_ssd_out.astype(BF16), w_f=w_fourier_out.astype(BF16),
        b_f=b_fourier_out.reshape(1, D_MODEL), w_o=w_out.astype(BF16),
        norm_ffn=norm_ffn.reshape(1, D_MODEL),
        w_gate=(0.5 * w_gate_up[:, :D_FF]).astype(BF16), w_up=w_gate_up[:, D_FF:].astype(BF16),
        w_down=w_down.astype(BF16), norm_final=norm_final.reshape(1, D_MODEL),
    )


def _trunk(x, p):
    bsz, seq, _ = x.shape
    t = bsz * seq
    bm = min(1024, t)
    x2d = x.reshape(t, D_MODEL)
    zs, xsc, bt, cc, gs, zr, zi, dtt = _inproj(x2d, p["norm_mix"], p["w_all"], p["w_dt"],
                                               _channel_dft_matrix(), p["conv_w"], p["conv_b"], bm, seq)
    y = _ssd(xsc, bt, cc, zs, dtt, p["bias_rep"], p["alog_rep"], p["dskip_exp"], p["ssd_norm"], bsz, seq)
    mixed = _dft(zr, zi, _dft_constants(seq), bsz, seq)
    x1 = _merge(y.reshape(t, D_INNER), mixed, gs, x2d, p["w_ssd"], p["w_f"], p["b_f"], p["w_o"],
                min(512, t))
    out = _ffn(x1, p["norm_ffn"], p["w_gate"], p["w_up"], p["w_down"], p["norm_final"], bm, D_FF // 2)
    return out.reshape(bsz, seq, D_MODEL)


def kernel(x_prompt, x_sample, norm_mix, w_in, conv_w, conv_b, dt_bias_f, dt_bias_b, a_log_f, a_log_b,
           d_skip, ssd_norm, w_ssd_out, w_fourier_out, b_fourier_out, w_out, norm_ffn, w_gate_up, w_down,
           norm_final):
    p = _prep_weights(norm_mix[0], w_in[0], conv_w[0], conv_b[0], dt_bias_f[0], dt_bias_b[0],
                      a_log_f[0], a_log_b[0], d_skip[0], ssd_norm[0], w_ssd_out[0], w_fourier_out[0],
                      b_fourier_out[0], w_out[0], norm_ffn[0], w_gate_up[0], w_down[0], norm_final)
    return (_trunk(x_prompt, p), _trunk(x_sample, p))
```

```python
import functools
import math

import numpy as np
import jax
import jax.numpy as jnp
from jax import lax
from jax.experimental import pallas as pl
from jax.experimental.pallas import tpu as pltpu

F32 = jnp.float32
BF16 = jnp.bfloat16

D_MODEL = 1024
D_INNER = 2048
HEAD_DIM = 64
SSD_HEADS = 32
SSD_GROUPS = 8
HEADS_PER_GROUP = 4
D_STATE = 128
CONV_WIDTH = 7
CONV_PAD = CONV_WIDTH // 2
CHUNK = 128
GROUP_CH = D_INNER // SSD_GROUPS
D_FOURIER = 1024
FOURIER_GROUP_DIM = 128
FOURIER_GROUPS = 8
D_FF = 2816
EPS = 1e-5

LANES = 128
SUBLANES = 8
BF16_ROWS = 16
DFT_N1 = 16
DFT_COLS = 256
PROJ_TILE = 1024
N_MAIN = 8192
VMEM_LIMIT = 56 * 1024 * 1024


def _silu_of_twice(t):
    return t + t * jnp.tanh(t)


def _sigmoid_of_twice(t):
    return 0.5 + 0.5 * jnp.tanh(t)


def _softplus(v):
    return jnp.maximum(v, 0.0) + jnp.log1p(jnp.exp(-jnp.abs(v)))


def _rms_scale(v):
    return v * lax.rsqrt(jnp.mean(v * v, axis=-1, keepdims=True) + EPS)


def _dot(a, b):
    return jnp.dot(a, b, preferred_element_type=F32)


TILE_XS = 2
TILE_B = 4
TILE_C = 5
TILE_GATE = 6
TILE_FOURIER = 8
HALO = BF16_ROWS


def _inproj_kernel(x_ref, xp_ref, xn_ref, g_ref, w_ref, wdt_ref, cs_ref, cw_ref, cb_ref,
                   zs_ref, xsc_ref, bt_ref, cc_ref, gs_ref, zr_ref, zi_ref, dtt_ref,
                   h_ref, slab_ref, conv_ref, *, tiles_per_seq):
    i = pl.program_id(0)
    j = pl.program_id(1)
    bm = x_ref.shape[0]
    main = pl.ds(HALO, bm)

    @pl.when(j == 0)
    def _():
        g = g_ref[...]
        h = (_rms_scale(x_ref[...]) * g).astype(BF16)
        h_ref[main, :] = h
        pos = i % tiles_per_seq
        hp = _rms_scale(xp_ref[0]) * g
        hn = _rms_scale(xn_ref[0]) * g
        h_ref[0:HALO, :] = jnp.where(pos == 0, 0.0, hp).astype(BF16)
        h_ref[HALO + bm:, :] = jnp.where(pos == tiles_per_seq - 1, 0.0, hn).astype(BF16)
        dtt_ref[...] = _dot(h, wdt_ref[...]).T

    @pl.when(j < TILE_XS)
    def _():
        zs_ref[...] = _silu_of_twice(_dot(h_ref[main, :], w_ref[...])).astype(BF16)

    @pl.when((j >= TILE_XS) & (j < TILE_GATE))
    def _():
        for q in range(PROJ_TILE // (2 * LANES)):
            pe = _dot(h_ref[...], w_ref[:, q * 2 * LANES:(q + 1) * 2 * LANES])
            for s in range(2 * q, 2 * q + 2):
                sl = slice(s * LANES, (s + 1) * LANES)
                slab_ref[s] = pe[:, (s - 2 * q) * LANES:(s - 2 * q + 1) * LANES]
                acc = cb_ref[:, sl]
                for k in range(CONV_WIDTH):
                    acc = acc + cw_ref[k:k + 1, sl] * slab_ref[s, HALO - CONV_PAD + k:HALO - CONV_PAD + k + bm, :]
                conv_ref[:, sl] = _silu_of_twice(acc).astype(BF16)

    @pl.when((j >= TILE_XS) & (j < TILE_B))
    def _():
        xsc_ref[...] = conv_ref[...]

    @pl.when(j == TILE_B)
    def _():
        for g in range(SSD_GROUPS):
            for c in range(bm // CHUNK):
                rows = slice(c * CHUNK, (c + 1) * CHUNK)
                blk = conv_ref[rows, g * D_STATE:(g + 1) * D_STATE].astype(F32)
                bt_ref[g, rows, :] = blk.T.astype(BF16)

    @pl.when(j == TILE_C)
    def _():
        cc_ref[...] = conv_ref[...]

    @pl.when((j >= TILE_GATE) & (j < TILE_FOURIER))
    def _():
        gs_ref[...] = _sigmoid_of_twice(_dot(h_ref[main, :], w_ref[...])).astype(BF16)

    @pl.when(j == TILE_FOURIER)
    def _():
        u = _dot(h_ref[main, :], w_ref[...]).astype(BF16)
        for g in range(FOURIER_GROUPS):
            sl = slice(g * FOURIER_GROUP_DIM, (g + 1) * FOURIER_GROUP_DIM)
            z = _dot(u[:, sl], cs_ref[...])
            zr_ref[:, sl] = z[:, :FOURIER_GROUP_DIM].astype(BF16)
            zi_ref[:, sl] = z[:, FOURIER_GROUP_DIM:].astype(BF16)


def _inproj(x2d, norm_g, w_all, w_dt, cs, conv_w, conv_b, bm, seq):
    t = x2d.shape[0]
    n_tiles = w_all.shape[1] // PROJ_TILE
    assert n_tiles == TILE_FOURIER + 1 and seq % bm == 0 and bm % HALO == 0
    halo_blocks = bm // HALO
    x_halo = x2d.reshape(t // HALO, HALO, D_MODEL)
    clip = lambda v, lo, hi: jnp.minimum(jnp.maximum(v, lo), hi)
    tok = lambda width, first, count: pl.BlockSpec(
        (bm, width), lambda i, j: (i, clip(j - first, 0, count - 1)))
    bf16_out = lambda cols: jax.ShapeDtypeStruct((t, cols), BF16)
    return pl.pallas_call(
        functools.partial(_inproj_kernel, tiles_per_seq=seq // bm),
        grid=(t // bm, n_tiles),
        in_specs=[
            pl.BlockSpec((bm, D_MODEL), lambda i, j: (i, 0)),
            pl.BlockSpec((1, HALO, D_MODEL), lambda i, j: (jnp.maximum(i * halo_blocks - 1, 0), 0, 0)),
            pl.BlockSpec((1, HALO, D_MODEL),
                         lambda i, j: (jnp.minimum((i + 1) * halo_blocks, t // HALO - 1), 0, 0)),
            pl.BlockSpec((1, D_MODEL), lambda i, j: (0, 0)),
            pl.BlockSpec((D_MODEL, PROJ_TILE), lambda i, j: (0, j)),
            pl.BlockSpec((D_MODEL, LANES), lambda i, j: (0, 0)),
            pl.BlockSpec((FOURIER_GROUP_DIM, 2 * FOURIER_GROUP_DIM), lambda i, j: (0, 0)),
            pl.BlockSpec((CONV_WIDTH, PROJ_TILE), lambda i, j: (0, clip(j - TILE_XS, 0, 3))),
            pl.BlockSpec((1, PROJ_TILE), lambda i, j: (0, clip(j - TILE_XS, 0, 3))),
        ],
        out_specs=[
            tok(PROJ_TILE, 0, 2),
            tok(PROJ_TILE, TILE_XS, 2),
            pl.BlockSpec((SSD_GROUPS, bm, D_STATE), lambda i, j: (0, i, 0)),
            tok(PROJ_TILE, TILE_C, 1),
            tok(PROJ_TILE, TILE_GATE, 2),
            tok(D_FOURIER, TILE_FOURIER, 1),
            tok(D_FOURIER, TILE_FOURIER, 1),
            pl.BlockSpec((LANES, bm), lambda i, j: (0, i)),
        ],
        out_shape=[
            bf16_out(D_INNER), bf16_out(D_INNER),
            jax.ShapeDtypeStruct((SSD_GROUPS, t, D_STATE), BF16),
            bf16_out(SSD_GROUPS * D_STATE), bf16_out(2 * D_MODEL),
            bf16_out(D_FOURIER), bf16_out(D_FOURIER),
            jax.ShapeDtypeStruct((LANES, t), F32),
        ],
        scratch_shapes=[
            pltpu.VMEM((bm + 2 * HALO, D_MODEL), BF16),
            pltpu.VMEM((PROJ_TILE // LANES, bm + 2 * HALO, LANES), F32),
            pltpu.VMEM((bm, PROJ_TILE), BF16),
        ],
        compiler_params=pltpu.CompilerParams(
            dimension_semantics=("arbitrary", "arbitrary"), vmem_limit_bytes=VMEM_LIMIT),
        name="inproj",
    )(x2d, x_halo, x_halo, norm_g, w_all, w_dt, cs, conv_w, conv_b)


Q_E = 48
N_PIECES = 3
HEAD_DIRS = 2 * HEADS_PER_GROUP
SSD_UNROLL = 8


def _split3(v):
    p0 = v.astype(BF16).astype(F32)
    r1 = v - p0
    p1 = r1.astype(BF16).astype(F32)
    return p0, p1, (r1 - p1).astype(BF16).astype(F32)


def _ssd_kernel(xsb_ref, bt_ref, cc_ref, zs_ref, dtt_ref,
                bias_ref, alog_ref, dsk_ref, nrm_ref, exp_ref,
                out_ref,
                y_ref, locf_ref, locb_ref, sinf_ref, sinb_ref,
                qt_ref, rows_ref, cdf_ref, cdb_ref, *, seq):
    n_chunks = seq // CHUNK

    def per_chunk_group(fn, width):
        def body(i, carry):
            for u in range(width):
                fn(width * i + u)
            return carry
        lax.fori_loop(0, n_chunks // width, body, 0)

    row_s = lax.broadcasted_iota(jnp.int32, (HEAD_DIRS, seq), 0)
    lane_s = lax.broadcasted_iota(jnp.int32, (HEAD_DIRS, seq), 1) & (CHUNK - 1)
    dt_all = _softplus(dtt_ref[...] + bias_ref[:, 0:1])
    dta = dt_all * (-jnp.exp(alog_ref[:, 0:1]))
    cf = dta
    rb = dta
    k = 1
    while k < CHUNK:
        cf = cf + jnp.where(lane_s >= k, pltpu.roll(cf, k, 1), 0.0)
        rb = rb + jnp.where(lane_s < CHUNK - k, pltpu.roll(rb, seq - k, 1), 0.0)
        k *= 2
    rows_ref[0] = dt_all
    rows_ref[1] = jnp.where(row_s < HEADS_PER_GROUP, cf, rb)

    dskip = dsk_ref[...]

    row8 = lax.broadcasted_iota(jnp.int32, (SUBLANES, LANES), 0)
    li = lax.broadcasted_iota(jnp.int32, (CHUNK, CHUNK), 0)
    si = lax.broadcasted_iota(jnp.int32, (CHUNK, CHUNK), 1)
    lane_g = lax.broadcasted_iota(jnp.int32, (CHUNK, GROUP_CH), 1)
    is_fwd_row = row8 < HEADS_PER_GROUP
    zero8 = jnp.zeros((SUBLANES, LANES), F32)
    lower_b = jnp.where(li >= si, 1.0, 0.0).astype(BF16)
    upper_b = jnp.where(li <= si, 1.0, 0.0).astype(BF16)

    def diag_pass(c):
        t0 = pl.multiple_of(c * CHUNK, CHUNK)
        cb = cc_ref[pl.ds(t0, CHUNK), :]
        bt = bt_ref[pl.ds(t0, CHUNK), :]
        xb = xsb_ref[pl.ds(t0, CHUNK), :]
        scores = _dot(cb, bt).astype(BF16)

        dt = rows_ref[0, :, pl.ds(t0, CHUNK)]
        cum = rows_ref[1, :, pl.ds(t0, CHUNK)]
        tot = jnp.where(is_fwd_row, cum[:, CHUNK - 1:CHUNK], cum[:, 0:1])
        wrow = dt * jnp.exp(tot - cum)
        ep = _split3(jnp.exp(cum))

        q = jnp.concatenate([zero8] * (Q_E // SUBLANES) + [ep[0], ep[1], ep[2]]
                            + [zero8] * ((CHUNK - Q_E) // SUBLANES - N_PIECES), axis=0)
        qtf = q.T
        qt_ref[c] = qtf.astype(BF16)

        edge = jnp.concatenate([qtf[0:SUBLANES, :], qtf[CHUNK - SUBLANES:, :]], axis=0).astype(BF16)
        cd = _dot(edge, exp_ref[...])
        cdf_ref[pl.ds(c, 1), :] = cd[2 * SUBLANES - 1:2 * SUBLANES, :GROUP_CH]
        cdb_ref[pl.ds(c, 1), :] = cd[0:1, GROUP_CH:]

        dtb = dt.astype(BF16)
        wb = wrow.astype(BF16)
        lhs_m = []
        lhs_f = []
        lhs_b = []
        xm = []
        for h in range(HEADS_PER_GROUP):
            hb = HEADS_PER_GROUP + h
            rf = jnp.broadcast_to(cum[h:h + 1, :], (CHUNK, CHUNK))
            rbk = jnp.broadcast_to(cum[hb:hb + 1, :], (CHUNK, CHUNK))
            arg = jnp.where(li >= si, rf.T - rf, rbk.T - rbk)
            coef = lower_b * dtb[h:h + 1, :] + upper_b * dtb[hb:hb + 1, :]
            lhs_m.append(scores * jnp.exp(arg).astype(BF16) * coef)
            lhs_f.append(bt * wb[h:h + 1, :])
            lhs_b.append(bt * wb[hb:hb + 1, :])
            in_head = (lane_g >= h * HEAD_DIM) & (lane_g < (h + 1) * HEAD_DIM)
            xm.append(jnp.where(in_head, xb, jnp.zeros_like(xb)))
        lhs = jnp.concatenate([jnp.concatenate(lhs_m, axis=1),
                               jnp.concatenate(lhs_f, axis=1),
                               jnp.concatenate(lhs_b, axis=1)], axis=0)
        big = _dot(lhs, jnp.concatenate(xm, axis=0))
        y_ref[pl.ds(t0, CHUNK), :] = big[:CHUNK] + dskip * xb.astype(F32)
        locf_ref[c] = big[CHUNK:2 * CHUNK]
        locb_ref[c] = big[2 * CHUNK:]

    per_chunk_group(diag_pass, 2 * SSD_UNROLL)

    def scan_fwd(i, state):
        sinf_ref[i] = state.astype(BF16)
        return state * cdf_ref[pl.ds(i, 1), :] + locf_ref[i]

    def scan_bwd(i, state):
        j = n_chunks - 1 - i
        sinb_ref[j] = state.astype(BF16)
        return state * cdb_ref[pl.ds(j, 1), :] + locb_ref[j]

    lax.fori_loop(0, n_chunks, scan_fwd, jnp.zeros((D_STATE, GROUP_CH), F32))
    lax.fori_loop(0, n_chunks, scan_bwd, jnp.zeros((D_STATE, GROUP_CH), F32))

    nrm = nrm_ref[...]

    def final_pass(c):
        t0 = pl.multiple_of(c * CHUNK, CHUNK)
        st = jnp.concatenate([sinf_ref[c], sinb_ref[c]], axis=1)
        yo = _dot(cc_ref[pl.ds(t0, CHUNK), :], st)
        ee = _dot(qt_ref[c], exp_ref[...])
        y = (y_ref[pl.ds(t0, CHUNK), :] + yo[:, :GROUP_CH] * ee[:, :GROUP_CH]
             + yo[:, GROUP_CH:] * ee[:, GROUP_CH:])
        y = y * zs_ref[pl.ds(t0, CHUNK), :].astype(F32)
        out_ref[pl.ds(t0, CHUNK), :] = (_rms_scale(y) * nrm).astype(BF16)

    per_chunk_group(final_pass, 2 * SSD_UNROLL)


def _expand_matrix():
    expand = np.zeros((CHUNK, 2 * GROUP_CH), np.float32)
    for r in range(N_PIECES):
        for j in range(HEAD_DIRS):
            expand[Q_E + SUBLANES * r + j, j * HEAD_DIM:(j + 1) * HEAD_DIM] = 1.0
    return jnp.asarray(expand).astype(BF16)


def _ssd(xsc, bt, cc, zs, dtt, bias_rep, alog_rep, dskip_exp, ssd_norm, bsz, seq):
    n_chunks = seq // CHUNK
    assert n_chunks % (2 * SSD_UNROLL) == 0
    expand = _expand_matrix()
    row_spec = lambda width: pl.BlockSpec((None, seq, width), lambda b, g: (b, 0, g))
    par_spec = lambda rows, width: pl.BlockSpec((rows, width), lambda b, g: (0, g))
    return pl.pallas_call(
        functools.partial(_ssd_kernel, seq=seq),
        grid=(bsz, SSD_GROUPS),
        in_specs=[
            row_spec(GROUP_CH),
            pl.BlockSpec((None, None, seq, D_STATE), lambda b, g: (g, b, 0, 0)),
            row_spec(D_STATE), row_spec(GROUP_CH),
            pl.BlockSpec((HEAD_DIRS, seq), lambda b, g: (g, b)),
            pl.BlockSpec((HEAD_DIRS, LANES), lambda b, g: (g, 0)),
            pl.BlockSpec((HEAD_DIRS, LANES), lambda b, g: (g, 0)),
            par_spec(1, GROUP_CH), par_spec(1, GROUP_CH),
            pl.BlockSpec(expand.shape, lambda b, g: (0, 0)),
        ],
        out_specs=pl.BlockSpec((None, seq, GROUP_CH), lambda b, g: (b, 0, g)),
        out_shape=jax.ShapeDtypeStruct((bsz, seq, D_INNER), BF16),
        scratch_shapes=[
            pltpu.VMEM((seq, GROUP_CH), F32),
            pltpu.VMEM((n_chunks, D_STATE, GROUP_CH), F32),
            pltpu.VMEM((n_chunks, D_STATE, GROUP_CH), F32),
            pltpu.VMEM((n_chunks, D_STATE, GROUP_CH), BF16),
            pltpu.VMEM((n_chunks, D_STATE, GROUP_CH), BF16),
            pltpu.VMEM((n_chunks, CHUNK, CHUNK), BF16),
            pltpu.VMEM((2, HEAD_DIRS, seq), F32),
            pltpu.VMEM((n_chunks, GROUP_CH), F32),
            pltpu.VMEM((n_chunks, GROUP_CH), F32),
        ],
        compiler_params=pltpu.CompilerParams(
            dimension_semantics=("arbitrary", "arbitrary"), vmem_limit_bytes=VMEM_LIMIT),
        name="ssd",
    )(xsc.reshape(bsz, seq, D_INNER), bt.reshape(SSD_GROUPS, bsz, seq, D_STATE),
      cc.reshape(bsz, seq, SSD_GROUPS * D_STATE), zs.reshape(bsz, seq, D_INNER), dtt,
      bias_rep, alog_rep, dskip_exp, ssd_norm, expand)


def _dft_kernel(zr_ref, zi_ref, a1_ref, a2_ref, a3_ref, o_ref, sr_ref, si_ref, *, n2):
    cols = zr_ref.shape[-1]
    rows1 = DFT_N1 * BF16_ROWS
    rows2 = n2 * BF16_ROWS

    def stack(r_ref, i_ref, idx, rows):
        return jnp.concatenate([r_ref[idx].reshape(rows, cols), i_ref[idx].reshape(rows, cols)], axis=0)

    for t2 in range(n2):
        y = _dot(a1_ref[...], stack(zr_ref, zi_ref, (slice(None), t2), rows1))
        sr_ref[:, t2] = y[:rows1].astype(BF16).reshape(DFT_N1, BF16_ROWS, cols)
        si_ref[:, t2] = y[rows1:].astype(BF16).reshape(DFT_N1, BF16_ROWS, cols)
    for t3 in range(BF16_ROWS):
        y = _dot(a2_ref[...], stack(sr_ref, si_ref, t3, rows2))
        sr_ref[t3] = y[:rows2].astype(BF16).reshape(n2, BF16_ROWS, cols)
        si_ref[t3] = y[rows2:].astype(BF16).reshape(n2, BF16_ROWS, cols)
    for k2 in range(n2):
        y = _dot(a3_ref[k2], stack(sr_ref, si_ref, (slice(None), k2), rows1))
        o_ref[:, k2] = y.astype(BF16).reshape(BF16_ROWS, BF16_ROWS, cols)


def _dft(zr, zi, consts, bsz, seq):
    a1, a2, a3 = consts
    n2 = seq // (DFT_N1 * BF16_ROWS)
    shape5 = (bsz, DFT_N1, n2, BF16_ROWS, D_FOURIER)
    spec = pl.BlockSpec((None, DFT_N1, n2, BF16_ROWS, DFT_COLS), lambda b, j: (b, 0, 0, 0, j))
    whole = lambda a: pl.BlockSpec(a.shape, lambda b, j: (0,) * a.ndim)
    out = pl.pallas_call(
        functools.partial(_dft_kernel, n2=n2),
        grid=(bsz, D_FOURIER // DFT_COLS),
        in_specs=[spec, spec, whole(a1), whole(a2), whole(a3)],
        out_specs=spec,
        out_shape=jax.ShapeDtypeStruct(shape5, BF16),
        scratch_shapes=[pltpu.VMEM((BF16_ROWS, n2, BF16_ROWS, DFT_COLS), BF16),
                        pltpu.VMEM((BF16_ROWS, n2, BF16_ROWS, DFT_COLS), BF16)],
        compiler_params=pltpu.CompilerParams(
            dimension_semantics=("arbitrary", "arbitrary"), vmem_limit_bytes=VMEM_LIMIT),
        name="dft",
    )(zr.reshape(shape5), zi.reshape(shape5), a1, a2, a3)
    return out.reshape(bsz * seq, D_FOURIER)


def _channel_dft_matrix():
    cidx = np.arange(FOURIER_GROUP_DIM)
    ang = 2.0 * np.pi * np.outer(cidx, cidx) / FOURIER_GROUP_DIM
    cs = np.concatenate([np.cos(ang), -np.sin(ang)], axis=1)
    return jnp.asarray(cs.astype(np.float32)).astype(BF16)


def _dft_constants(seq):
    n1, n3 = DFT_N1, BF16_ROWS
    n2 = seq // (n1 * n3)
    w = lambda n, e: np.exp(-2j * np.pi * (np.asarray(e) % n) / n)
    i16 = np.arange(n3)
    eye = np.eye(n3)
    m1 = np.einsum("kt,ab->aktb", w(n1, np.outer(np.arange(n1), np.arange(n1))), eye).reshape(n3 * n1, n1 * n3)
    k2 = np.arange(n2)
    f2 = w(n2, np.outer(k2, k2))[:, :, None] * w(n1 * n2, np.outer(k2, np.arange(n1)))[None, :, :]
    m2 = np.einsum("ktc,cd->kctd", f2, np.eye(n1)).reshape(n2 * n1, n2 * n1)
    f3 = (w(n3, np.outer(i16, i16))[None, :, :, None]
          * w(n2 * n3, np.outer(k2, i16))[:, None, :, None]
          * w(seq, np.outer(i16, np.arange(n1)))[None, None, :, :])
    m3 = np.einsum("jktc,cd->jkctd", f3, np.eye(n1)).reshape(n2, n3 * n1, n3 * n1)
    m3 = m3 / math.sqrt(seq * FOURIER_GROUP_DIM)
    full = lambda m: np.concatenate([np.concatenate([m.real, -m.imag], axis=-1),
                                     np.concatenate([m.imag, m.real], axis=-1)], axis=-2)
    real_part = lambda m: np.concatenate([m.real, -m.imag], axis=-1)
    to_bf16 = lambda a: jnp.asarray(a.astype(np.float32)).astype(BF16)
    return to_bf16(full(m1)), to_bf16(full(m2)), to_bf16(real_part(m3))


def _merge_kernel(y_ref, mx_ref, gs_ref, x_ref, wssd_ref, wf_ref, bf_ref, wo_ref, o_ref):
    a_out = _dot(y_ref[...], wssd_ref[...])
    f_out = _dot(mx_ref[...], wf_ref[...]) + bf_ref[...]
    gates = gs_ref[...].astype(F32)
    merged = (gates[:, :D_MODEL] * a_out + gates[:, D_MODEL:] * f_out).astype(BF16)
    o_ref[...] = x_ref[...] + _dot(merged, wo_ref[...])


def _merge(y2d, mixed, gs, x2d, w_ssd, w_f, b_f, w_o, bm):
    t = x2d.shape[0]
    const = lambda r, c: pl.BlockSpec((r, c), lambda i: (0, 0))
    return pl.pallas_call(
        _merge_kernel,
        grid=(t // bm,),
        in_specs=[
            pl.BlockSpec((bm, D_INNER), lambda i: (i, 0)),
            pl.BlockSpec((bm, D_FOURIER), lambda i: (i, 0)),
            pl.BlockSpec((bm, 2 * D_MODEL), lambda i: (i, 0)),
            pl.BlockSpec((bm, D_MODEL), lambda i: (i, 0)),
            const(D_INNER, D_MODEL), const(D_FOURIER, D_MODEL), const(1, D_MODEL), const(D_MODEL, D_MODEL),
        ],
        out_specs=pl.BlockSpec((bm, D_MODEL), lambda i: (i, 0)),
        out_shape=jax.ShapeDtypeStruct((t, D_MODEL), F32),
        compiler_params=pltpu.CompilerParams(
            dimension_semantics=("arbitrary",), vmem_limit_bytes=VMEM_LIMIT),
        name="merge",
    )(y2d, mixed, gs, x2d, w_ssd, w_f, b_f, w_o)


def _ffn_kernel(x_ref, g_ref, wg_ref, wu_ref, wd_ref, gfin_ref, o_ref, *, ff_tile):
    x = x_ref[...]
    h = (_rms_scale(x) * g_ref[...]).astype(BF16)
    y = x
    for f in range(D_FF // ff_tile):
        cols = slice(f * ff_tile, (f + 1) * ff_tile)
        act = (_silu_of_twice(_dot(h, wg_ref[:, cols])) * _dot(h, wu_ref[:, cols])).astype(BF16)
        y = y + _dot(act, wd_ref[cols, :])
    o_ref[...] = _rms_scale(y) * gfin_ref[...]


def _ffn(x1, norm_g, w_gate, w_up, w_down, norm_fin, bm, ff_tile):
    t = x1.shape[0]
    resident = lambda a: pl.BlockSpec(a.shape, lambda i: (0, 0), pipeline_mode=pl.Buffered(1))
    return pl.pallas_call(
        functools.partial(_ffn_kernel, ff_tile=ff_tile),
        grid=(t // bm,),
        in_specs=[
            pl.BlockSpec((bm, D_MODEL), lambda i: (i, 0)),
            resident(norm_g), resident(w_gate), resident(w_up), resident(w_down), resident(norm_fin),
        ],
        out_specs=pl.BlockSpec((bm, D_MODEL), lambda i: (i, 0)),
        out_shape=jax.ShapeDtypeStruct((t, D_MODEL), F32),
        compiler_params=pltpu.CompilerParams(
            dimension_semantics=("arbitrary",), vmem_limit_bytes=VMEM_LIMIT),
        name="ffn",
    )(x1, norm_g, w_gate, w_up, w_down, norm_fin)


def _prep_weights(norm_mix, w_in, conv_w, conv_b, dt_bias_f, dt_bias_b, a_log_f, a_log_b, d_skip,
                  ssd_norm, w_ssd_out, w_fourier_out, b_fourier_out, w_out, norm_ffn, w_gate_up,
                  w_down, norm_final):
    o_dt = D_INNER + (D_INNER + 2 * SSD_GROUPS * D_STATE)
    o_u = o_dt + 2 * SSD_HEADS
    o_g = o_u + D_FOURIER
    w_all = jnp.concatenate([0.5 * w_in[:, :D_INNER], w_in[:, D_INNER:o_dt], 0.5 * w_in[:, o_g:],
                             w_in[:, o_u:o_g]], axis=1).astype(BF16)

    def by_group(f, b):
        return jnp.concatenate([f.reshape(SSD_GROUPS, HEADS_PER_GROUP),
                                b.reshape(SSD_GROUPS, HEADS_PER_GROUP)], axis=1).reshape(-1)

    w_dt_cols = w_in[:, o_dt:o_u]
    w_dt = jnp.concatenate([w_dt_cols[:, :SSD_HEADS].reshape(D_MODEL, SSD_GROUPS, HEADS_PER_GROUP),
                            w_dt_cols[:, SSD_HEADS:].reshape(D_MODEL, SSD_GROUPS, HEADS_PER_GROUP)],
                           axis=2).reshape(D_MODEL, 2 * SSD_HEADS)
    w_dt = jnp.pad(w_dt, ((0, 0), (0, LANES - 2 * SSD_HEADS))).astype(BF16)
    rep = lambda v: jnp.broadcast_to(v.astype(F32)[:, None], (2 * SSD_HEADS, LANES))
    return dict(
        norm_mix=norm_mix.reshape(1, D_MODEL), w_all=w_all, w_dt=w_dt,
        conv_w=0.5 * conv_w, conv_b=0.5 * conv_b.reshape(1, -1),
        bias_rep=rep(by_group(dt_bias_f, dt_bias_b)), alog_rep=rep(by_group(a_log_f, a_log_b)),
        dskip_exp=jnp.repeat(d_skip.astype(F32), HEAD_DIM).reshape(1, D_INNER),
        ssd_norm=ssd_norm.reshape(1, D_INNER),
        w_ssd=w_ssd_out.astype(BF16), w_f=w_fourier_out.astype(BF16),
        b_f=b_fourier_out.reshape(1, D_MODEL), w_o=w_out.astype(BF16),
        norm_ffn=norm_ffn.reshape(1, D_MODEL),
        w_gate=(0.5 * w_gate_up[:, :D_FF]).astype(BF16), w_up=w_gate_up[:, D_FF:].astype(BF16),
        w_down=w_down.astype(BF16), norm_final=norm_final.reshape(1, D_MODEL),
    )


def _trunk(x, p):
    bsz, seq, _ = x.shape
    t = bsz * seq
    bm = min(1024, t)
    x2d = x.reshape(t, D_MODEL)
    zs, xsc, bt, cc, gs, zr, zi, dtt = _inproj(x2d, p["norm_mix"], p["w_all"], p["w_dt"],
                                               _channel_dft_matrix(), p["conv_w"], p["conv_b"], bm, seq)
    y = _ssd(xsc, bt, cc, zs, dtt, p["bias_rep"], p["alog_rep"], p["dskip_exp"], p["ssd_norm"], bsz, seq)
    mixed = _dft(zr, zi, _dft_constants(seq), bsz, seq)
    x1 = _merge(y.reshape(t, D_INNER), mixed, gs, x2d, p["w_ssd"], p["w_f"], p["b_f"], p["w_o"],
                min(512, t))
    out = _ffn(x1, p["norm_ffn"], p["w_gate"], p["w_up"], p["w_down"], p["norm_final"], min(512, t),
               D_FF // 2)
    return out.reshape(bsz, seq, D_MODEL)


def kernel(x_prompt, x_sample, norm_mix, w_in, conv_w, conv_b, dt_bias_f, dt_bias_b, a_log_f, a_log_b,
           d_skip, ssd_norm, w_ssd_out, w_fourier_out, b_fourier_out, w_out, norm_ffn, w_gate_up, w_down,
           norm_final):
    p = _prep_weights(norm_mix[0], w_in[0], conv_w[0], conv_b[0], dt_bias_f[0], dt_bias_b[0],
                      a_log_f[0], a_log_b[0], d_skip[0], ssd_norm[0], w_ssd_out[0], w_fourier_out[0],
                      b_fourier_out[0], w_out[0], norm_ffn[0], w_gate_up[0], w_down[0], norm_final)
    return (_trunk(x_prompt, p), _trunk(x_sample, p))
```

```python
import functools
import math

import numpy as np
import jax
import jax.numpy as jnp
from jax import lax
from jax.experimental import pallas as pl
from jax.experimental.pallas import tpu as pltpu

F32 = jnp.float32
BF16 = jnp.bfloat16

D_MODEL = 1024
D_INNER = 2048
HEAD_DIM = 64
SSD_HEADS = 32
SSD_GROUPS = 8
HEADS_PER_GROUP = 4
D_STATE = 128
CONV_WIDTH = 7
CONV_PAD = CONV_WIDTH // 2
CHUNK = 128
GROUP_CH = D_INNER // SSD_GROUPS
D_FOURIER = 1024
FOURIER_GROUP_DIM = 128
FOURIER_GROUPS = 8
D_FF = 2816
EPS = 1e-5

LANES = 128
SUBLANES = 8
BF16_ROWS = 16
DFT_N1 = 16
DFT_COLS = 256
PROJ_TILE = 1024
N_MAIN = 8192
VMEM_LIMIT = 56 * 1024 * 1024


def _silu_of_twice(t):
    return t + t * jnp.tanh(t)


def _sigmoid_of_twice(t):
    return 0.5 + 0.5 * jnp.tanh(t)


def _softplus(v):
    return jnp.maximum(v, 0.0) + jnp.log1p(jnp.exp(-jnp.abs(v)))


def _rms_scale(v):
    return v * lax.rsqrt(jnp.mean(v * v, axis=-1, keepdims=True) + EPS)


def _dot(a, b):
    return jnp.dot(a, b, preferred_element_type=F32)


TILE_XS = 2
TILE_B = 4
TILE_C = 5
TILE_GATE = 6
TILE_FOURIER = 8
HALO = BF16_ROWS
PROJ_ROW_BLOCKS = 6
CONV_ROW_BLOCKS = 8


def _inproj_kernel(x_ref, xp_ref, xn_ref, g_ref, w_ref, wdt_ref, cs_ref, cw_ref, cb_ref,
                   zs_ref, xsc_ref, bt_ref, cc_ref, gs_ref, zr_ref, zi_ref, dtt_ref,
                   h_ref, slab_ref, conv_ref, *, tiles_per_seq):
    i = pl.program_id(0)
    j = pl.program_id(1)
    bm = x_ref.shape[0]
    main = pl.ds(HALO, bm)

    @pl.when(j == 0)
    def _():
        g = g_ref[...]
        h = (_rms_scale(x_ref[...]) * g).astype(BF16)
        h_ref[main, :] = h
        pos = i % tiles_per_seq
        hp = _rms_scale(xp_ref[0]) * g
        hn = _rms_scale(xn_ref[0]) * g
        h_ref[0:HALO, :] = jnp.where(pos == 0, 0.0, hp).astype(BF16)
        h_ref[HALO + bm:, :] = jnp.where(pos == tiles_per_seq - 1, 0.0, hn).astype(BF16)
        dtt_ref[...] = _dot(h, wdt_ref[...]).T

    @pl.when(j < TILE_XS)
    def _():
        zs_ref[...] = _silu_of_twice(_dot(h_ref[main, :], w_ref[...])).astype(BF16)

    @pl.when((j >= TILE_XS) & (j < TILE_GATE))
    def _():
        n_pb, n_rb = PROJ_ROW_BLOCKS, CONV_ROW_BLOCKS
        rb_rows = (bm + 2 * HALO) // n_pb
        ob_rows = bm // n_rb
        for q in range(PROJ_TILE // (2 * LANES)):
            for r in range(n_pb):
                rows = slice(r * rb_rows, (r + 1) * rb_rows)
                pe = _dot(h_ref[rows, :], w_ref[:, q * 2 * LANES:(q + 1) * 2 * LANES])
                slab_ref[2 * q, rows, :] = pe[:, :LANES]
                slab_ref[2 * q + 1, rows, :] = pe[:, LANES:]
            for s in range(2 * q, 2 * q + 2):
                sl = slice(s * LANES, (s + 1) * LANES)
                for r in range(n_rb):
                    acc = cb_ref[:, sl]
                    for k in range(CONV_WIDTH):
                        lo = HALO - CONV_PAD + k + r * ob_rows
                        acc = acc + cw_ref[k:k + 1, sl] * slab_ref[s, lo:lo + ob_rows, :]
                    conv_ref[r * ob_rows:(r + 1) * ob_rows, sl] = _silu_of_twice(acc).astype(BF16)

    @pl.when((j >= TILE_XS) & (j < TILE_B))
    def _():
        xsc_ref[...] = conv_ref[...]

    @pl.when(j == TILE_B)
    def _():
        for g in range(SSD_GROUPS):
            for c in range(bm // CHUNK):
                rows = slice(c * CHUNK, (c + 1) * CHUNK)
                blk = conv_ref[rows, g * D_STATE:(g + 1) * D_STATE].astype(F32)
                bt_ref[g, rows, :] = blk.T.astype(BF16)

    @pl.when(j == TILE_C)
    def _():
        cc_ref[...] = conv_ref[...]

    @pl.when((j >= TILE_GATE) & (j < TILE_FOURIER))
    def _():
        gs_ref[...] = _sigmoid_of_twice(_dot(h_ref[main, :], w_ref[...])).astype(BF16)

    @pl.when(j == TILE_FOURIER)
    def _():
        u = _dot(h_ref[main, :], w_ref[...]).astype(BF16)
        for g in range(FOURIER_GROUPS):
            sl = slice(g * FOURIER_GROUP_DIM, (g + 1) * FOURIER_GROUP_DIM)
            z = _dot(u[:, sl], cs_ref[...])
            zr_ref[:, sl] = z[:, :FOURIER_GROUP_DIM].astype(BF16)
            zi_ref[:, sl] = z[:, FOURIER_GROUP_DIM:].astype(BF16)


def _inproj(x2d, norm_g, w_all, w_dt, cs, conv_w, conv_b, bm, seq):
    t = x2d.shape[0]
    n_tiles = w_all.shape[1] // PROJ_TILE
    assert n_tiles == TILE_FOURIER + 1 and seq % bm == 0 and bm % HALO == 0
    halo_blocks = bm // HALO
    x_halo = x2d.reshape(t // HALO, HALO, D_MODEL)
    clip = lambda v, lo, hi: jnp.minimum(jnp.maximum(v, lo), hi)
    tok = lambda width, first, count: pl.BlockSpec(
        (bm, width), lambda i, j: (i, clip(j - first, 0, count - 1)))
    bf16_out = lambda cols: jax.ShapeDtypeStruct((t, cols), BF16)
    return pl.pallas_call(
        functools.partial(_inproj_kernel, tiles_per_seq=seq // bm),
        grid=(t // bm, n_tiles),
        in_specs=[
            pl.BlockSpec((bm, D_MODEL), lambda i, j: (i, 0)),
            pl.BlockSpec((1, HALO, D_MODEL), lambda i, j: (jnp.maximum(i * halo_blocks - 1, 0), 0, 0)),
            pl.BlockSpec((1, HALO, D_MODEL),
                         lambda i, j: (jnp.minimum((i + 1) * halo_blocks, t // HALO - 1), 0, 0)),
            pl.BlockSpec((1, D_MODEL), lambda i, j: (0, 0)),
            pl.BlockSpec((D_MODEL, PROJ_TILE), lambda i, j: (0, j)),
            pl.BlockSpec((D_MODEL, LANES), lambda i, j: (0, 0)),
            pl.BlockSpec((FOURIER_GROUP_DIM, 2 * FOURIER_GROUP_DIM), lambda i, j: (0, 0)),
            pl.BlockSpec((CONV_WIDTH, PROJ_TILE), lambda i, j: (0, clip(j - TILE_XS, 0, 3))),
            pl.BlockSpec((1, PROJ_TILE), lambda i, j: (0, clip(j - TILE_XS, 0, 3))),
        ],
        out_specs=[
            tok(PROJ_TILE, 0, 2),
            tok(PROJ_TILE, TILE_XS, 2),
            pl.BlockSpec((SSD_GROUPS, bm, D_STATE), lambda i, j: (0, i, 0)),
            tok(PROJ_TILE, TILE_C, 1),
            tok(PROJ_TILE, TILE_GATE, 2),
            tok(D_FOURIER, TILE_FOURIER, 1),
            tok(D_FOURIER, TILE_FOURIER, 1),
            pl.BlockSpec((LANES, bm), lambda i, j: (0, i)),
        ],
        out_shape=[
            bf16_out(D_INNER), bf16_out(D_INNER),
            jax.ShapeDtypeStruct((SSD_GROUPS, t, D_STATE), BF16),
            bf16_out(SSD_GROUPS * D_STATE), bf16_out(2 * D_MODEL),
            bf16_out(D_FOURIER), bf16_out(D_FOURIER),
            jax.ShapeDtypeStruct((LANES, t), F32),
        ],
        scratch_shapes=[
            pltpu.VMEM((bm + 2 * HALO, D_MODEL), BF16),
            pltpu.VMEM((PROJ_TILE // LANES, bm + 2 * HALO, LANES), F32),
            pltpu.VMEM((bm, PROJ_TILE), BF16),
        ],
        compiler_params=pltpu.CompilerParams(
            dimension_semantics=("arbitrary", "arbitrary"), vmem_limit_bytes=VMEM_LIMIT),
        name="inproj",
    )(x2d, x_halo, x_halo, norm_g, w_all, w_dt, cs, conv_w, conv_b)


Q_E = 48
N_PIECES = 3
HEAD_DIRS = 2 * HEADS_PER_GROUP
SSD_UNROLL = 8


def _split3(v):
    p0 = v.astype(BF16).astype(F32)
    r1 = v - p0
    p1 = r1.astype(BF16).astype(F32)
    return p0, p1, (r1 - p1).astype(BF16).astype(F32)


def _ssd_kernel(xsb_ref, bt_ref, cc_ref, zs_ref, dtt_ref,
                bias_ref, alog_ref, dsk_ref, nrm_ref, exp_ref,
                out_ref,
                y_ref, locf_ref, locb_ref, sinf_ref, sinb_ref,
                qt_ref, rows_ref, cdf_ref, cdb_ref, *, seq):
    n_chunks = seq // CHUNK

    def per_chunk_group(fn, width):
        def body(i, carry):
            for u in range(width):
                fn(width * i + u)
            return carry
        lax.fori_loop(0, n_chunks // width, body, 0)

    row_s = lax.broadcasted_iota(jnp.int32, (HEAD_DIRS, seq), 0)
    lane_s = lax.broadcasted_iota(jnp.int32, (HEAD_DIRS, seq), 1) & (CHUNK - 1)
    dt_all = _softplus(dtt_ref[...] + bias_ref[:, 0:1])
    dta = dt_all * (-jnp.exp(alog_ref[:, 0:1]))
    cf = dta
    rb = dta
    k = 1
    while k < CHUNK:
        cf = cf + jnp.where(lane_s >= k, pltpu.roll(cf, k, 1), 0.0)
        rb = rb + jnp.where(lane_s < CHUNK - k, pltpu.roll(rb, seq - k, 1), 0.0)
        k *= 2
    rows_ref[0] = dt_all
    rows_ref[1] = jnp.where(row_s < HEADS_PER_GROUP, cf, rb)

    dskip = dsk_ref[...]

    row8 = lax.broadcasted_iota(jnp.int32, (SUBLANES, LANES), 0)
    li = lax.broadcasted_iota(jnp.int32, (CHUNK, CHUNK), 0)
    si = lax.broadcasted_iota(jnp.int32, (CHUNK, CHUNK), 1)
    lane_g = lax.broadcasted_iota(jnp.int32, (CHUNK, GROUP_CH), 1)
    is_fwd_row = row8 < HEADS_PER_GROUP
    zero8 = jnp.zeros((SUBLANES, LANES), F32)
    lower_b = jnp.where(li >= si, 1.0, 0.0).astype(BF16)
    upper_b = jnp.where(li <= si, 1.0, 0.0).astype(BF16)

    def diag_pass(c):
        t0 = pl.multiple_of(c * CHUNK, CHUNK)
        cb = cc_ref[pl.ds(t0, CHUNK), :]
        bt = bt_ref[pl.ds(t0, CHUNK), :]
        xb = xsb_ref[pl.ds(t0, CHUNK), :]
        scores = _dot(cb, bt).astype(BF16)

        dt = rows_ref[0, :, pl.ds(t0, CHUNK)]
        cum = rows_ref[1, :, pl.ds(t0, CHUNK)]
        tot = jnp.where(is_fwd_row, cum[:, CHUNK - 1:CHUNK], cum[:, 0:1])
        wrow = dt * jnp.exp(tot - cum)
        ep = _split3(jnp.exp(cum))

        q = jnp.concatenate([zero8] * (Q_E // SUBLANES) + [ep[0], ep[1], ep[2]]
                            + [zero8] * ((CHUNK - Q_E) // SUBLANES - N_PIECES), axis=0)
        qtf = q.T
        qt_ref[c] = qtf.astype(BF16)

        edge = jnp.concatenate([qtf[0:SUBLANES, :], qtf[CHUNK - SUBLANES:, :]], axis=0).astype(BF16)
        cd = _dot(edge, exp_ref[...])
        cdf_ref[pl.ds(c, 1), :] = cd[2 * SUBLANES - 1:2 * SUBLANES, :GROUP_CH]
        cdb_ref[pl.ds(c, 1), :] = cd[0:1, GROUP_CH:]

        dtb = dt.astype(BF16)
        wb = wrow.astype(BF16)
        lhs_m = []
        lhs_f = []
        lhs_b = []
        xm = []
        for h in range(HEADS_PER_GROUP):
            hb = HEADS_PER_GROUP + h
            rf = jnp.broadcast_to(cum[h:h + 1, :], (CHUNK, CHUNK))
            rbk = jnp.broadcast_to(cum[hb:hb + 1, :], (CHUNK, CHUNK))
            arg = jnp.where(li >= si, rf.T - rf, rbk.T - rbk)
            coef = lower_b * dtb[h:h + 1, :] + upper_b * dtb[hb:hb + 1, :]
            lhs_m.append(scores * jnp.exp(arg).astype(BF16) * coef)
            lhs_f.append(bt * wb[h:h + 1, :])
            lhs_b.append(bt * wb[hb:hb + 1, :])
            in_head = (lane_g >= h * HEAD_DIM) & (lane_g < (h + 1) * HEAD_DIM)
            xm.append(jnp.where(in_head, xb, jnp.zeros_like(xb)))
        lhs = jnp.concatenate([jnp.concatenate(lhs_m, axis=1),
                               jnp.concatenate(lhs_f, axis=1),
                               jnp.concatenate(lhs_b, axis=1)], axis=0)
        big = _dot(lhs, jnp.concatenate(xm, axis=0))
        y_ref[pl.ds(t0, CHUNK), :] = big[:CHUNK] + dskip * xb.astype(F32)
        locf_ref[c] = big[CHUNK:2 * CHUNK]
        locb_ref[c] = big[2 * CHUNK:]

    per_chunk_group(diag_pass, 2 * SSD_UNROLL)

    def scan_fwd(i, state):
        sinf_ref[i] = state.astype(BF16)
        return state * cdf_ref[pl.ds(i, 1), :] + locf_ref[i]

    def scan_bwd(i, state):
        j = n_chunks - 1 - i
        sinb_ref[j] = state.astype(BF16)
        return state * cdb_ref[pl.ds(j, 1), :] + locb_ref[j]

    lax.fori_loop(0, n_chunks, scan_fwd, jnp.zeros((D_STATE, GROUP_CH), F32))
    lax.fori_loop(0, n_chunks, scan_bwd, jnp.zeros((D_STATE, GROUP_CH), F32))

    nrm = nrm_ref[...]

    def final_pass(c):
        t0 = pl.multiple_of(c * CHUNK, CHUNK)
        st = jnp.concatenate([sinf_ref[c], sinb_ref[c]], axis=1)
        yo = _dot(cc_ref[pl.ds(t0, CHUNK), :], st)
        ee = _dot(qt_ref[c], exp_ref[...])
        y = (y_ref[pl.ds(t0, CHUNK), :] + yo[:, :GROUP_CH] * ee[:, :GROUP_CH]
             + yo[:, GROUP_CH:] * ee[:, GROUP_CH:])
        y = y * zs_ref[pl.ds(t0, CHUNK), :].astype(F32)
        out_ref[pl.ds(t0, CHUNK), :] = (_rms_scale(y) * nrm).astype(BF16)

    per_chunk_group(final_pass, 2 * SSD_UNROLL)


def _expand_matrix():
    expand = np.zeros((CHUNK, 2 * GROUP_CH), np.float32)
    for r in range(N_PIECES):
        for j in range(HEAD_DIRS):
            expand[Q_E + SUBLANES * r + j, j * HEAD_DIM:(j + 1) * HEAD_DIM] = 1.0
    return jnp.asarray(expand).astype(BF16)


def _ssd(xsc, bt, cc, zs, dtt, bias_rep, alog_rep, dskip_exp, ssd_norm, bsz, seq):
    n_chunks = seq // CHUNK
    assert n_chunks % (2 * SSD_UNROLL) == 0
    expand = _expand_matrix()
    row_spec = lambda width: pl.BlockSpec((None, seq, width), lambda b, g: (b, 0, g))
    par_spec = lambda rows, width: pl.BlockSpec((rows, width), lambda b, g: (0, g))
    return pl.pallas_call(
        functools.partial(_ssd_kernel, seq=seq),
        grid=(bsz, SSD_GROUPS),
        in_specs=[
            row_spec(GROUP_CH),
            pl.BlockSpec((None, None, seq, D_STATE), lambda b, g: (g, b, 0, 0)),
            row_spec(D_STATE), row_spec(GROUP_CH),
            pl.BlockSpec((HEAD_DIRS, seq), lambda b, g: (g, b)),
            pl.BlockSpec((HEAD_DIRS, LANES), lambda b, g: (g, 0)),
            pl.BlockSpec((HEAD_DIRS, LANES), lambda b, g: (g, 0)),
            par_spec(1, GROUP_CH), par_spec(1, GROUP_CH),
            pl.BlockSpec(expand.shape, lambda b, g: (0, 0)),
        ],
        out_specs=pl.BlockSpec((None, seq, GROUP_CH), lambda b, g: (b, 0, g)),
        out_shape=jax.ShapeDtypeStruct((bsz, seq, D_INNER), BF16),
        scratch_shapes=[
            pltpu.VMEM((seq, GROUP_CH), F32),
            pltpu.VMEM((n_chunks, D_STATE, GROUP_CH), F32),
            pltpu.VMEM((n_chunks, D_STATE, GROUP_CH), F32),
            pltpu.VMEM((n_chunks, D_STATE, GROUP_CH), BF16),
            pltpu.VMEM((n_chunks, D_STATE, GROUP_CH), BF16),
            pltpu.VMEM((n_chunks, CHUNK, CHUNK), BF16),
            pltpu.VMEM((2, HEAD_DIRS, seq), F32),
            pltpu.VMEM((n_chunks, GROUP_CH), F32),
            pltpu.VMEM((n_chunks, GROUP_CH), F32),
        ],
        compiler_params=pltpu.CompilerParams(
            dimension_semantics=("arbitrary", "arbitrary"), vmem_limit_bytes=VMEM_LIMIT),
        name="ssd",
    )(xsc.reshape(bsz, seq, D_INNER), bt.reshape(SSD_GROUPS, bsz, seq, D_STATE),
      cc.reshape(bsz, seq, SSD_GROUPS * D_STATE), zs.reshape(bsz, seq, D_INNER), dtt,
      bias_rep, alog_rep, dskip_exp, ssd_norm, expand)


def _dft_kernel(zr_ref, zi_ref, a1_ref, a2_ref, a3_ref, o_ref, sr_ref, si_ref, *, n2):
    cols = zr_ref.shape[-1]
    rows1 = DFT_N1 * BF16_ROWS
    rows2 = n2 * BF16_ROWS

    def stack(r_ref, i_ref, idx, rows):
        return jnp.concatenate([r_ref[idx].reshape(rows, cols), i_ref[idx].reshape(rows, cols)], axis=0)

    for t2 in range(n2):
        y = _dot(a1_ref[...], stack(zr_ref, zi_ref, (slice(None), t2), rows1))
        sr_ref[:, t2] = y[:rows1].astype(BF16).reshape(DFT_N1, BF16_ROWS, cols)
        si_ref[:, t2] = y[rows1:].astype(BF16).reshape(DFT_N1, BF16_ROWS, cols)
    for t3 in range(BF16_ROWS):
        y = _dot(a2_ref[...], stack(sr_ref, si_ref, t3, rows2))
        sr_ref[t3] = y[:rows2].astype(BF16).reshape(n2, BF16_ROWS, cols)
        si_ref[t3] = y[rows2:].astype(BF16).reshape(n2, BF16_ROWS, cols)
    for k2 in range(n2):
        y = _dot(a3_ref[k2], stack(sr_ref, si_ref, (slice(None), k2), rows1))
        o_ref[:, k2] = y.astype(BF16).reshape(BF16_ROWS, BF16_ROWS, cols)


def _dft(zr, zi, consts, bsz, seq):
    a1, a2, a3 = consts
    n2 = seq // (DFT_N1 * BF16_ROWS)
    shape5 = (bsz, DFT_N1, n2, BF16_ROWS, D_FOURIER)
    spec = pl.BlockSpec((None, DFT_N1, n2, BF16_ROWS, DFT_COLS), lambda b, j: (b, 0, 0, 0, j))
    whole = lambda a: pl.BlockSpec(a.shape, lambda b, j: (0,) * a.ndim)
    out = pl.pallas_call(
        functools.partial(_dft_kernel, n2=n2),
        grid=(bsz, D_FOURIER // DFT_COLS),
        in_specs=[spec, spec, whole(a1), whole(a2), whole(a3)],
        out_specs=spec,
        out_shape=jax.ShapeDtypeStruct(shape5, BF16),
        scratch_shapes=[pltpu.VMEM((BF16_ROWS, n2, BF16_ROWS, DFT_COLS), BF16),
                        pltpu.VMEM((BF16_ROWS, n2, BF16_ROWS, DFT_COLS), BF16)],
        compiler_params=pltpu.CompilerParams(
            dimension_semantics=("arbitrary", "arbitrary"), vmem_limit_bytes=VMEM_LIMIT),
        name="dft",
    )(zr.reshape(shape5), zi.reshape(shape5), a1, a2, a3)
    return out.reshape(bsz * seq, D_FOURIER)


def _channel_dft_matrix():
    cidx = np.arange(FOURIER_GROUP_DIM)
    ang = 2.0 * np.pi * np.outer(cidx, cidx) / FOURIER_GROUP_DIM
    cs = np.concatenate([np.cos(ang), -np.sin(ang)], axis=1)
    return jnp.asarray(cs.astype(np.float32)).astype(BF16)


def _dft_constants(seq):
    n1, n3 = DFT_N1, BF16_ROWS
    n2 = seq // (n1 * n3)
    w = lambda n, e: np.exp(-2j * np.pi * (np.asarray(e) % n) / n)
    i16 = np.arange(n3)
    eye = np.eye(n3)
    m1 = np.einsum("kt,ab->aktb", w(n1, np.outer(np.arange(n1), np.arange(n1))), eye).reshape(n3 * n1, n1 * n3)
    k2 = np.arange(n2)
    f2 = w(n2, np.outer(k2, k2))[:, :, None] * w(n1 * n2, np.outer(k2, np.arange(n1)))[None, :, :]
    m2 = np.einsum("ktc,cd->kctd", f2, np.eye(n1)).reshape(n2 * n1, n2 * n1)
    f3 = (w(n3, np.outer(i16, i16))[None, :, :, None]
          * w(n2 * n3, np.outer(k2, i16))[:, None, :, None]
          * w(seq, np.outer(i16, np.arange(n1)))[None, None, :, :])
    m3 = np.einsum("jktc,cd->jkctd", f3, np.eye(n1)).reshape(n2, n3 * n1, n3 * n1)
    m3 = m3 / math.sqrt(seq * FOURIER_GROUP_DIM)
    full = lambda m: np.concatenate([np.concatenate([m.real, -m.imag], axis=-1),
                                     np.concatenate([m.imag, m.real], axis=-1)], axis=-2)
    real_part = lambda m: np.concatenate([m.real, -m.imag], axis=-1)
    to_bf16 = lambda a: jnp.asarray(a.astype(np.float32)).astype(BF16)
    return to_bf16(full(m1)), to_bf16(full(m2)), to_bf16(real_part(m3))


def _merge_kernel(y_ref, mx_ref, gs_ref, x_ref, wssd_ref, wf_ref, bf_ref, wo_ref, o_ref):
    a_out = _dot(y_ref[...], wssd_ref[...])
    f_out = _dot(mx_ref[...], wf_ref[...]) + bf_ref[...]
    gates = gs_ref[...].astype(F32)
    merged = (gates[:, :D_MODEL] * a_out + gates[:, D_MODEL:] * f_out).astype(BF16)
    o_ref[...] = x_ref[...] + _dot(merged, wo_ref[...])


def _merge(y2d, mixed, gs, x2d, w_ssd, w_f, b_f, w_o, bm):
    t = x2d.shape[0]
    const = lambda r, c: pl.BlockSpec((r, c), lambda i: (0, 0))
    return pl.pallas_call(
        _merge_kernel,
        grid=(t // bm,),
        in_specs=[
            pl.BlockSpec((bm, D_INNER), lambda i: (i, 0)),
            pl.BlockSpec((bm, D_FOURIER), lambda i: (i, 0)),
            pl.BlockSpec((bm, 2 * D_MODEL), lambda i: (i, 0)),
            pl.BlockSpec((bm, D_MODEL), lambda i: (i, 0)),
            const(D_INNER, D_MODEL), const(D_FOURIER, D_MODEL), const(1, D_MODEL), const(D_MODEL, D_MODEL),
        ],
        out_specs=pl.BlockSpec((bm, D_MODEL), lambda i: (i, 0)),
        out_shape=jax.ShapeDtypeStruct((t, D_MODEL), F32),
        compiler_params=pltpu.CompilerParams(
            dimension_semantics=("arbitrary",), vmem_limit_bytes=VMEM_LIMIT),
        name="merge",
    )(y2d, mixed, gs, x2d, w_ssd, w_f, b_f, w_o)


def _ffn_kernel(x_ref, g_ref, wg_ref, wu_ref, wd_ref, gfin_ref, o_ref, *, ff_tile):
    x = x_ref[...]
    h = (_rms_scale(x) * g_ref[...]).astype(BF16)
    y = x
    for f in range(D_FF // ff_tile):
        cols = slice(f * ff_tile, (f + 1) * ff_tile)
        act = (_silu_of_twice(_dot(h, wg_ref[:, cols])) * _dot(h, wu_ref[:, cols])).astype(BF16)
        y = y + _dot(act, wd_ref[cols, :])
    o_ref[...] = _rms_scale(y) * gfin_ref[...]


def _ffn(x1, norm_g, w_gate, w_up, w_down, norm_fin, bm, ff_tile):
    t = x1.shape[0]
    resident = lambda a: pl.BlockSpec(a.shape, lambda i: (0, 0), pipeline_mode=pl.Buffered(1))
    return pl.pallas_call(
        functools.partial(_ffn_kernel, ff_tile=ff_tile),
        grid=(t // bm,),
        in_specs=[
            pl.BlockSpec((bm, D_MODEL), lambda i: (i, 0)),
            resident(norm_g), resident(w_gate), resident(w_up), resident(w_down), resident(norm_fin),
        ],
        out_specs=pl.BlockSpec((bm, D_MODEL), lambda i: (i, 0)),
        out_shape=jax.ShapeDtypeStruct((t, D_MODEL), F32),
        compiler_params=pltpu.CompilerParams(
            dimension_semantics=("arbitrary",), vmem_limit_bytes=VMEM_LIMIT),
        name="ffn",
    )(x1, norm_g, w_gate, w_up, w_down, norm_fin)


def _prep_weights(norm_mix, w_in, conv_w, conv_b, dt_bias_f, dt_bias_b, a_log_f, a_log_b, d_skip,
                  ssd_norm, w_ssd_out, w_fourier_out, b_fourier_out, w_out, norm_ffn, w_gate_up,
                  w_down, norm_final):
    o_dt = D_INNER + (D_INNER + 2 * SSD_GROUPS * D_STATE)
    o_u = o_dt + 2 * SSD_HEADS
    o_g = o_u + D_FOURIER
    w_all = jnp.concatenate([0.5 * w_in[:, :D_INNER], w_in[:, D_INNER:o_dt], 0.5 * w_in[:, o_g:],
                             w_in[:, o_u:o_g]], axis=1).astype(BF16)

    def by_group(f, b):
        return jnp.concatenate([f.reshape(SSD_GROUPS, HEADS_PER_GROUP),
                                b.reshape(SSD_GROUPS, HEADS_PER_GROUP)], axis=1).reshape(-1)

    w_dt_cols = w_in[:, o_dt:o_u]
    w_dt = jnp.concatenate([w_dt_cols[:, :SSD_HEADS].reshape(D_MODEL, SSD_GROUPS, HEADS_PER_GROUP),
                            w_dt_cols[:, SSD_HEADS:].reshape(D_MODEL, SSD_GROUPS, HEADS_PER_GROUP)],
                           axis=2).reshape(D_MODEL, 2 * SSD_HEADS)
    w_dt = jnp.pad(w_dt, ((0, 0), (0, LANES - 2 * SSD_HEADS))).astype(BF16)
    rep = lambda v: jnp.broadcast_to(v.astype(F32)[:, None], (2 * SSD_HEADS, LANES))
    return dict(
        norm_mix=norm_mix.reshape(1, D_MODEL), w_all=w_all, w_dt=w_dt,
        conv_w=0.5 * conv_w, conv_b=0.5 * conv_b.reshape(1, -1),
        bias_rep=rep(by_group(dt_bias_f, dt_bias_b)), alog_rep=rep(by_group(a_log_f, a_log_b)),
        dskip_exp=jnp.repeat(d_skip.astype(F32), HEAD_DIM).reshape(1, D_INNER),
        ssd_norm=ssd_norm.reshape(1, D_INNER),
        w_ssd=w_ssd_out.astype(BF16), w_f=w_fourier_out.astype(BF16),
        b_f=b_fourier_out.reshape(1, D_MODEL), w_o=w_out.astype(BF16),
        norm_ffn=norm_ffn.reshape(1, D_MODEL),
        w_gate=(0.5 * w_gate_up[:, :D_FF]).astype(BF16), w_up=w_gate_up[:, D_FF:].astype(BF16),
        w_down=w_down.astype(BF16), norm_final=norm_final.reshape(1, D_MODEL),
    )


def _trunk(x, p):
    bsz, seq, _ = x.shape
    t = bsz * seq
    bm = min(1024, t)
    x2d = x.reshape(t, D_MODEL)
    zs, xsc, bt, cc, gs, zr, zi, dtt = _inproj(x2d, p["norm_mix"], p["w_all"], p["w_dt"],
                                               _channel_dft_matrix(), p["conv_w"], p["conv_b"], bm, seq)
    y = _ssd(xsc, bt, cc, zs, dtt, p["bias_rep"], p["alog_rep"], p["dskip_exp"], p["ssd_norm"], bsz, seq)
    mixed = _dft(zr, zi, _dft_constants(seq), bsz, seq)
    x1 = _merge(y.reshape(t, D_INNER), mixed, gs, x2d, p["w_ssd"], p["w_f"], p["b_f"], p["w_o"],
                min(512, t))
    out = _ffn(x1, p["norm_ffn"], p["w_gate"], p["w_up"], p["w_down"], p["norm_final"], min(512, t),
               D_FF // 2)
    return out.reshape(bsz, seq, D_MODEL)


def kernel(x_prompt, x_sample, norm_mix, w_in, conv_w, conv_b, dt_bias_f, dt_bias_b, a_log_f, a_log_b,
           d_skip, ssd_norm, w_ssd_out, w_fourier_out, b_fourier_out, w_out, norm_ffn, w_gate_up, w_down,
           norm_final):
    p = _prep_weights(norm_mix[0], w_in[0], conv_w[0], conv_b[0], dt_bias_f[0], dt_bias_b[0],
                      a_log_f[0], a_log_b[0], d_skip[0], ssd_norm[0], w_ssd_out[0], w_fourier_out[0],
                      b_fourier_out[0], w_out[0], norm_ffn[0], w_gate_up[0], w_down[0], norm_final)
    return (_trunk(x_prompt, p), _trunk(x_sample, p))
```

```python
import functools
import math

import numpy as np
import jax
import jax.numpy as jnp
from jax import lax
from jax.experimental import pallas as pl
from jax.experimental.pallas import tpu as pltpu

F32 = jnp.float32
BF16 = jnp.bfloat16

D_MODEL = 1024
D_INNER = 2048
HEAD_DIM = 64
SSD_HEADS = 32
SSD_GROUPS = 8
HEADS_PER_GROUP = 4
D_STATE = 128
CONV_WIDTH = 7
CONV_PAD = CONV_WIDTH // 2
CHUNK = 128
GROUP_CH = D_INNER // SSD_GROUPS
D_FOURIER = 1024
FOURIER_GROUP_DIM = 128
FOURIER_GROUPS = 8
D_FF = 2816
EPS = 1e-5

LANES = 128
SUBLANES = 8
BF16_ROWS = 16
DFT_N1 = 16
DFT_COLS = 256
PROJ_TILE = 1024
N_MAIN = 8192
VMEM_LIMIT = 56 * 1024 * 1024


def _silu_of_twice(t):
    return t + t * jnp.tanh(t)


def _sigmoid_of_twice(t):
    return 0.5 + 0.5 * jnp.tanh(t)


def _softplus(v):
    return jnp.maximum(v, 0.0) + jnp.log1p(jnp.exp(-jnp.abs(v)))


def _rms_scale(v):
    return v * lax.rsqrt(jnp.mean(v * v, axis=-1, keepdims=True) + EPS)


def _dot(a, b):
    return jnp.dot(a, b, preferred_element_type=F32)


TILE_XS = 2
TILE_B = 4
TILE_C = 5
TILE_GATE = 6
TILE_FOURIER = 8
HALO = BF16_ROWS
PROJ_ROW_BLOCKS = 2
CONV_ROW_BLOCKS = 8


def _inproj_kernel(x_ref, xp_ref, xn_ref, g_ref, w_ref, wdt_ref, cs_ref, cw_ref, cb_ref,
                   zs_ref, xsc_ref, bt_ref, cc_ref, gs_ref, zr_ref, zi_ref, dtt_ref,
                   h_ref, slab_ref, conv_ref, *, tiles_per_seq):
    i = pl.program_id(0)
    j = pl.program_id(1)
    bm = x_ref.shape[0]
    main = pl.ds(HALO, bm)

    @pl.when(j == 0)
    def _():
        g = g_ref[...]
        h = (_rms_scale(x_ref[...]) * g).astype(BF16)
        h_ref[main, :] = h
        pos = i % tiles_per_seq
        hp = _rms_scale(xp_ref[0]) * g
        hn = _rms_scale(xn_ref[0]) * g
        h_ref[0:HALO, :] = jnp.where(pos == 0, 0.0, hp).astype(BF16)
        h_ref[HALO + bm:, :] = jnp.where(pos == tiles_per_seq - 1, 0.0, hn).astype(BF16)
        dtt_ref[...] = _dot(h, wdt_ref[...]).T

    @pl.when(j < TILE_XS)
    def _():
        zs_ref[...] = _silu_of_twice(_dot(h_ref[main, :], w_ref[...])).astype(BF16)

    @pl.when((j >= TILE_XS) & (j < TILE_GATE))
    def _():
        n_pb, n_rb = PROJ_ROW_BLOCKS, CONV_ROW_BLOCKS
        rb_rows = (bm + 2 * HALO) // n_pb
        ob_rows = bm // n_rb
        for q in range(PROJ_TILE // (2 * LANES)):
            for r in range(n_pb):
                rows = slice(r * rb_rows, (r + 1) * rb_rows)
                pe = _dot(h_ref[rows, :], w_ref[:, q * 2 * LANES:(q + 1) * 2 * LANES])
                slab_ref[2 * q, rows, :] = pe[:, :LANES]
                slab_ref[2 * q + 1, rows, :] = pe[:, LANES:]
            for s in range(2 * q, 2 * q + 2):
                sl = slice(s * LANES, (s + 1) * LANES)
                for r in range(n_rb):
                    acc = cb_ref[:, sl]
                    for k in range(CONV_WIDTH):
                        lo = HALO - CONV_PAD + k + r * ob_rows
                        acc = acc + cw_ref[k:k + 1, sl] * slab_ref[s, lo:lo + ob_rows, :]
                    conv_ref[r * ob_rows:(r + 1) * ob_rows, sl] = _silu_of_twice(acc).astype(BF16)

    @pl.when((j >= TILE_XS) & (j < TILE_B))
    def _():
        xsc_ref[...] = conv_ref[...]

    @pl.when(j == TILE_B)
    def _():
        for g in range(SSD_GROUPS):
            for c in range(bm // CHUNK):
                rows = slice(c * CHUNK, (c + 1) * CHUNK)
                blk = conv_ref[rows, g * D_STATE:(g + 1) * D_STATE].astype(F32)
                bt_ref[g, rows, :] = blk.T.astype(BF16)

    @pl.when(j == TILE_C)
    def _():
        cc_ref[...] = conv_ref[...]

    @pl.when((j >= TILE_GATE) & (j < TILE_FOURIER))
    def _():
        gs_ref[...] = _sigmoid_of_twice(_dot(h_ref[main, :], w_ref[...])).astype(BF16)

    @pl.when(j == TILE_FOURIER)
    def _():
        u = _dot(h_ref[main, :], w_ref[...]).astype(BF16)
        for g in range(FOURIER_GROUPS):
            sl = slice(g * FOURIER_GROUP_DIM, (g + 1) * FOURIER_GROUP_DIM)
            z = _dot(u[:, sl], cs_ref[...])
            zr_ref[:, sl] = z[:, :FOURIER_GROUP_DIM].astype(BF16)
            zi_ref[:, sl] = z[:, FOURIER_GROUP_DIM:].astype(BF16)


def _inproj(x2d, norm_g, w_all, w_dt, cs, conv_w, conv_b, bm, seq):
    t = x2d.shape[0]
    n_tiles = w_all.shape[1] // PROJ_TILE
    assert n_tiles == TILE_FOURIER + 1 and seq % bm == 0 and bm % HALO == 0
    halo_blocks = bm // HALO
    x_halo = x2d.reshape(t // HALO, HALO, D_MODEL)
    clip = lambda v, lo, hi: jnp.minimum(jnp.maximum(v, lo), hi)
    tok = lambda width, first, count: pl.BlockSpec(
        (bm, width), lambda i, j: (i, clip(j - first, 0, count - 1)))
    bf16_out = lambda cols: jax.ShapeDtypeStruct((t, cols), BF16)
    return pl.pallas_call(
        functools.partial(_inproj_kernel, tiles_per_seq=seq // bm),
        grid=(t // bm, n_tiles),
        in_specs=[
            pl.BlockSpec((bm, D_MODEL), lambda i, j: (i, 0)),
            pl.BlockSpec((1, HALO, D_MODEL), lambda i, j: (jnp.maximum(i * halo_blocks - 1, 0), 0, 0)),
            pl.BlockSpec((1, HALO, D_MODEL),
                         lambda i, j: (jnp.minimum((i + 1) * halo_blocks, t // HALO - 1), 0, 0)),
            pl.BlockSpec((1, D_MODEL), lambda i, j: (0, 0)),
            pl.BlockSpec((D_MODEL, PROJ_TILE), lambda i, j: (0, j)),
            pl.BlockSpec((D_MODEL, LANES), lambda i, j: (0, 0)),
            pl.BlockSpec((FOURIER_GROUP_DIM, 2 * FOURIER_GROUP_DIM), lambda i, j: (0, 0)),
            pl.BlockSpec((CONV_WIDTH, PROJ_TILE), lambda i, j: (0, clip(j - TILE_XS, 0, 3))),
            pl.BlockSpec((1, PROJ_TILE), lambda i, j: (0, clip(j - TILE_XS, 0, 3))),
        ],
        out_specs=[
            tok(PROJ_TILE, 0, 2),
            tok(PROJ_TILE, TILE_XS, 2),
            pl.BlockSpec((SSD_GROUPS, bm, D_STATE), lambda i, j: (0, i, 0)),
            tok(PROJ_TILE, TILE_C, 1),
            tok(PROJ_TILE, TILE_GATE, 2),
            tok(D_FOURIER, TILE_FOURIER, 1),
            tok(D_FOURIER, TILE_FOURIER, 1),
            pl.BlockSpec((LANES, bm), lambda i, j: (0, i)),
        ],
        out_shape=[
            bf16_out(D_INNER), bf16_out(D_INNER),
            jax.ShapeDtypeStruct((SSD_GROUPS, t, D_STATE), BF16),
            bf16_out(SSD_GROUPS * D_STATE), bf16_out(2 * D_MODEL),
            bf16_out(D_FOURIER), bf16_out(D_FOURIER),
            jax.ShapeDtypeStruct((LANES, t), F32),
        ],
        scratch_shapes=[
            pltpu.VMEM((bm + 2 * HALO, D_MODEL), BF16),
            pltpu.VMEM((PROJ_TILE // LANES, bm + 2 * HALO, LANES), F32),
            pltpu.VMEM((bm, PROJ_TILE), BF16),
        ],
        compiler_params=pltpu.CompilerParams(
            dimension_semantics=("arbitrary", "arbitrary"), vmem_limit_bytes=VMEM_LIMIT),
        name="inproj",
    )(x2d, x_halo, x_halo, norm_g, w_all, w_dt, cs, conv_w, conv_b)


Q_E = 48
N_PIECES = 3
HEAD_DIRS = 2 * HEADS_PER_GROUP
SSD_UNROLL = 8


def _split3(v):
    p0 = v.astype(BF16).astype(F32)
    r1 = v - p0
    p1 = r1.astype(BF16).astype(F32)
    return p0, p1, (r1 - p1).astype(BF16).astype(F32)


def _ssd_kernel(xsb_ref, bt_ref, cc_ref, zs_ref, dtt_ref,
                bias_ref, alog_ref, dsk_ref, nrm_ref, exp_ref,
                out_ref,
                y_ref, locf_ref, locb_ref, sinf_ref, sinb_ref,
                qt_ref, rows_ref, cdf_ref, cdb_ref, *, seq):
    n_chunks = seq // CHUNK

    def per_chunk_group(fn, width):
        def body(i, carry):
            for u in range(width):
                fn(width * i + u)
            return carry
        lax.fori_loop(0, n_chunks // width, body, 0)

    row_s = lax.broadcasted_iota(jnp.int32, (HEAD_DIRS, seq), 0)
    lane_s = lax.broadcasted_iota(jnp.int32, (HEAD_DIRS, seq), 1) & (CHUNK - 1)
    dt_all = _softplus(dtt_ref[...] + bias_ref[:, 0:1])
    dta = dt_all * (-jnp.exp(alog_ref[:, 0:1]))
    cf = dta
    rb = dta
    k = 1
    while k < CHUNK:
        cf = cf + jnp.where(lane_s >= k, pltpu.roll(cf, k, 1), 0.0)
        rb = rb + jnp.where(lane_s < CHUNK - k, pltpu.roll(rb, seq - k, 1), 0.0)
        k *= 2
    rows_ref[0] = dt_all
    rows_ref[1] = jnp.where(row_s < HEADS_PER_GROUP, cf, rb)

    dskip = dsk_ref[...]

    row8 = lax.broadcasted_iota(jnp.int32, (SUBLANES, LANES), 0)
    li = lax.broadcasted_iota(jnp.int32, (CHUNK, CHUNK), 0)
    si = lax.broadcasted_iota(jnp.int32, (CHUNK, CHUNK), 1)
    lane_g = lax.broadcasted_iota(jnp.int32, (CHUNK, GROUP_CH), 1)
    is_fwd_row = row8 < HEADS_PER_GROUP
    zero8 = jnp.zeros((SUBLANES, LANES), F32)
    lower_b = jnp.where(li >= si, 1.0, 0.0).astype(BF16)
    upper_b = jnp.where(li <= si, 1.0, 0.0).astype(BF16)

    def diag_pass(c):
        t0 = pl.multiple_of(c * CHUNK, CHUNK)
        cb = cc_ref[pl.ds(t0, CHUNK), :]
        bt = bt_ref[pl.ds(t0, CHUNK), :]
        xb = xsb_ref[pl.ds(t0, CHUNK), :]
        scores = _dot(cb, bt).astype(BF16)

        dt = rows_ref[0, :, pl.ds(t0, CHUNK)]
        cum = rows_ref[1, :, pl.ds(t0, CHUNK)]
        tot = jnp.where(is_fwd_row, cum[:, CHUNK - 1:CHUNK], cum[:, 0:1])
        wrow = dt * jnp.exp(tot - cum)
        ep = _split3(jnp.exp(cum))

        q = jnp.concatenate([zero8] * (Q_E // SUBLANES) + [ep[0], ep[1], ep[2]]
                            + [zero8] * ((CHUNK - Q_E) // SUBLANES - N_PIECES), axis=0)
        qtf = q.T
        qt_ref[c] = qtf.astype(BF16)

        edge = jnp.concatenate([qtf[0:SUBLANES, :], qtf[CHUNK - SUBLANES:, :]], axis=0).astype(BF16)
        cd = _dot(edge, exp_ref[...])
        cdf_ref[pl.ds(c, 1), :] = cd[2 * SUBLANES - 1:2 * SUBLANES, :GROUP_CH]
        cdb_ref[pl.ds(c, 1), :] = cd[0:1, GROUP_CH:]

        dtb = dt.astype(BF16)
        wb = wrow.astype(BF16)
        lhs_m = []
        lhs_f = []
        lhs_b = []
        xm = []
        for h in range(HEADS_PER_GROUP):
            hb = HEADS_PER_GROUP + h
            rf = jnp.broadcast_to(cum[h:h + 1, :], (CHUNK, CHUNK))
            rbk = jnp.broadcast_to(cum[hb:hb + 1, :], (CHUNK, CHUNK))
            arg = jnp.where(li >= si, rf.T - rf, rbk.T - rbk)
            coef = lower_b * dtb[h:h + 1, :] + upper_b * dtb[hb:hb + 1, :]
            lhs_m.append(scores * jnp.exp(arg).astype(BF16) * coef)
            lhs_f.append(bt * wb[h:h + 1, :])
            lhs_b.append(bt * wb[hb:hb + 1, :])
            in_head = (lane_g >= h * HEAD_DIM) & (lane_g < (h + 1) * HEAD_DIM)
            xm.append(jnp.where(in_head, xb, jnp.zeros_like(xb)))
        lhs = jnp.concatenate([jnp.concatenate(lhs_m, axis=1),
                               jnp.concatenate(lhs_f, axis=1),
                               jnp.concatenate(lhs_b, axis=1)], axis=0)
        big = _dot(lhs, jnp.concatenate(xm, axis=0))
        y_ref[pl.ds(t0, CHUNK), :] = big[:CHUNK] + dskip * xb.astype(F32)
        locf_ref[c] = big[CHUNK:2 * CHUNK]
        locb_ref[c] = big[2 * CHUNK:]

    per_chunk_group(diag_pass, 2 * SSD_UNROLL)

    def scan_fwd(i, state):
        sinf_ref[i] = state.astype(BF16)
        return state * cdf_ref[pl.ds(i, 1), :] + locf_ref[i]

    def scan_bwd(i, state):
        j = n_chunks - 1 - i
        sinb_ref[j] = state.astype(BF16)
        return state * cdb_ref[pl.ds(j, 1), :] + locb_ref[j]

    lax.fori_loop(0, n_chunks, scan_fwd, jnp.zeros((D_STATE, GROUP_CH), F32))
    lax.fori_loop(0, n_chunks, scan_bwd, jnp.zeros((D_STATE, GROUP_CH), F32))

    nrm = nrm_ref[...]

    def final_pass(c):
        t0 = pl.multiple_of(c * CHUNK, CHUNK)
        st = jnp.concatenate([sinf_ref[c], sinb_ref[c]], axis=1)
        yo = _dot(cc_ref[pl.ds(t0, CHUNK), :], st)
        ee = _dot(qt_ref[c], exp_ref[...])
        y = (y_ref[pl.ds(t0, CHUNK), :] + yo[:, :GROUP_CH] * ee[:, :GROUP_CH]
             + yo[:, GROUP_CH:] * ee[:, GROUP_CH:])
        y = y * zs_ref[pl.ds(t0, CHUNK), :].astype(F32)
        out_ref[pl.ds(t0, CHUNK), :] = (_rms_scale(y) * nrm).astype(BF16)

    per_chunk_group(final_pass, 2 * SSD_UNROLL)


def _expand_matrix():
    expand = np.zeros((CHUNK, 2 * GROUP_CH), np.float32)
    for r in range(N_PIECES):
        for j in range(HEAD_DIRS):
            expand[Q_E + SUBLANES * r + j, j * HEAD_DIM:(j + 1) * HEAD_DIM] = 1.0
    return jnp.asarray(expand).astype(BF16)


def _ssd(xsc, bt, cc, zs, dtt, bias_rep, alog_rep, dskip_exp, ssd_norm, bsz, seq):
    n_chunks = seq // CHUNK
    assert n_chunks % (2 * SSD_UNROLL) == 0
    expand = _expand_matrix()
    row_spec = lambda width: pl.BlockSpec((None, seq, width), lambda b, g: (b, 0, g))
    par_spec = lambda rows, width: pl.BlockSpec((rows, width), lambda b, g: (0, g))
    return pl.pallas_call(
        functools.partial(_ssd_kernel, seq=seq),
        grid=(bsz, SSD_GROUPS),
        in_specs=[
            row_spec(GROUP_CH),
            pl.BlockSpec((None, None, seq, D_STATE), lambda b, g: (g, b, 0, 0)),
            row_spec(D_STATE), row_spec(GROUP_CH),
            pl.BlockSpec((HEAD_DIRS, seq), lambda b, g: (g, b)),
            pl.BlockSpec((HEAD_DIRS, LANES), lambda b, g: (g, 0)),
            pl.BlockSpec((HEAD_DIRS, LANES), lambda b, g: (g, 0)),
            par_spec(1, GROUP_CH), par_spec(1, GROUP_CH),
            pl.BlockSpec(expand.shape, lambda b, g: (0, 0)),
        ],
        out_specs=pl.BlockSpec((None, seq, GROUP_CH), lambda b, g: (b, 0, g)),
        out_shape=jax.ShapeDtypeStruct((bsz, seq, D_INNER), BF16),
        scratch_shapes=[
            pltpu.VMEM((seq, GROUP_CH), F32),
            pltpu.VMEM((n_chunks, D_STATE, GROUP_CH), F32),
            pltpu.VMEM((n_chunks, D_STATE, GROUP_CH), F32),
            pltpu.VMEM((n_chunks, D_STATE, GROUP_CH), BF16),
            pltpu.VMEM((n_chunks, D_STATE, GROUP_CH), BF16),
            pltpu.VMEM((n_chunks, CHUNK, CHUNK), BF16),
            pltpu.VMEM((2, HEAD_DIRS, seq), F32),
            pltpu.VMEM((n_chunks, GROUP_CH), F32),
            pltpu.VMEM((n_chunks, GROUP_CH), F32),
        ],
        compiler_params=pltpu.CompilerParams(
            dimension_semantics=("arbitrary", "arbitrary"), vmem_limit_bytes=VMEM_LIMIT),
        name="ssd",
    )(xsc.reshape(bsz, seq, D_INNER), bt.reshape(SSD_GROUPS, bsz, seq, D_STATE),
      cc.reshape(bsz, seq, SSD_GROUPS * D_STATE), zs.reshape(bsz, seq, D_INNER), dtt,
      bias_rep, alog_rep, dskip_exp, ssd_norm, expand)


def _dft_kernel(zr_ref, zi_ref, a1_ref, a2_ref, a3_ref, o_ref, sr_ref, si_ref, *, n2):
    cols = zr_ref.shape[-1]
    rows1 = DFT_N1 * BF16_ROWS
    rows2 = n2 * BF16_ROWS

    def stack(r_ref, i_ref, idx, rows):
        return jnp.concatenate([r_ref[idx].reshape(rows, cols), i_ref[idx].reshape(rows, cols)], axis=0)

    for t2 in range(n2):
        y = _dot(a1_ref[...], stack(zr_ref, zi_ref, (slice(None), t2), rows1))
        sr_ref[:, t2] = y[:rows1].astype(BF16).reshape(DFT_N1, BF16_ROWS, cols)
        si_ref[:, t2] = y[rows1:].astype(BF16).reshape(DFT_N1, BF16_ROWS, cols)
    for t3 in range(BF16_ROWS):
        y = _dot(a2_ref[...], stack(sr_ref, si_ref, t3, rows2))
        sr_ref[t3] = y[:rows2].astype(BF16).reshape(n2, BF16_ROWS, cols)
        si_ref[t3] = y[rows2:].astype(BF16).reshape(n2, BF16_ROWS, cols)
    for k2 in range(n2):
        y = _dot(a3_ref[k2], stack(sr_ref, si_ref, (slice(None), k2), rows1))
        o_ref[:, k2] = y.astype(BF16).reshape(BF16_ROWS, BF16_ROWS, cols)


def _dft(zr, zi, consts, bsz, seq):
    a1, a2, a3 = consts
    n2 = seq // (DFT_N1 * BF16_ROWS)
    shape5 = (bsz, DFT_N1, n2, BF16_ROWS, D_FOURIER)
    spec = pl.BlockSpec((None, DFT_N1, n2, BF16_ROWS, DFT_COLS), lambda b, j: (b, 0, 0, 0, j))
    whole = lambda a: pl.BlockSpec(a.shape, lambda b, j: (0,) * a.ndim)
    out = pl.pallas_call(
        functools.partial(_dft_kernel, n2=n2),
        grid=(bsz, D_FOURIER // DFT_COLS),
        in_specs=[spec, spec, whole(a1), whole(a2), whole(a3)],
        out_specs=spec,
        out_shape=jax.ShapeDtypeStruct(shape5, BF16),
        scratch_shapes=[pltpu.VMEM((BF16_ROWS, n2, BF16_ROWS, DFT_COLS), BF16),
                        pltpu.VMEM((BF16_ROWS, n2, BF16_ROWS, DFT_COLS), BF16)],
        compiler_params=pltpu.CompilerParams(
            dimension_semantics=("arbitrary", "arbitrary"), vmem_limit_bytes=VMEM_LIMIT),
        name="dft",
    )(zr.reshape(shape5), zi.reshape(shape5), a1, a2, a3)
    return out.reshape(bsz * seq, D_FOURIER)


def _channel_dft_matrix():
    cidx = np.arange(FOURIER_GROUP_DIM)
    ang = 2.0 * np.pi * np.outer(cidx, cidx) / FOURIER_GROUP_DIM
    cs = np.concatenate([np.cos(ang), -np.sin(ang)], axis=1)
    return jnp.asarray(cs.astype(np.float32)).astype(BF16)


def _dft_constants(seq):
    n1, n3 = DFT_N1, BF16_ROWS
    n2 = seq // (n1 * n3)
    w = lambda n, e: np.exp(-2j * np.pi * (np.asarray(e) % n) / n)
    i16 = np.arange(n3)
    eye = np.eye(n3)
    m1 = np.einsum("kt,ab->aktb", w(n1, np.outer(np.arange(n1), np.arange(n1))), eye).reshape(n3 * n1, n1 * n3)
    k2 = np.arange(n2)
    f2 = w(n2, np.outer(k2, k2))[:, :, None] * w(n1 * n2, np.outer(k2, np.arange(n1)))[None, :, :]
    m2 = np.einsum("ktc,cd->kctd", f2, np.eye(n1)).reshape(n2 * n1, n2 * n1)
    f3 = (w(n3, np.outer(i16, i16))[None, :, :, None]
          * w(n2 * n3, np.outer(k2, i16))[:, None, :, None]
          * w(seq, np.outer(i16, np.arange(n1)))[None, None, :, :])
    m3 = np.einsum("jktc,cd->jkctd", f3, np.eye(n1)).reshape(n2, n3 * n1, n3 * n1)
    m3 = m3 / math.sqrt(seq * FOURIER_GROUP_DIM)
    full = lambda m: np.concatenate([np.concatenate([m.real, -m.imag], axis=-1),
                                     np.concatenate([m.imag, m.real], axis=-1)], axis=-2)
    real_part = lambda m: np.concatenate([m.real, -m.imag], axis=-1)
    to_bf16 = lambda a: jnp.asarray(a.astype(np.float32)).astype(BF16)
    return to_bf16(full(m1)), to_bf16(full(m2)), to_bf16(real_part(m3))


def _merge_kernel(y_ref, mx_ref, gs_ref, x_ref, wssd_ref, wf_ref, bf_ref, wo_ref, o_ref):
    a_out = _dot(y_ref[...], wssd_ref[...])
    f_out = _dot(mx_ref[...], wf_ref[...]) + bf_ref[...]
    gates = gs_ref[...].astype(F32)
    merged = (gates[:, :D_MODEL] * a_out + gates[:, D_MODEL:] * f_out).astype(BF16)
    o_ref[...] = x_ref[...] + _dot(merged, wo_ref[...])


def _merge(y2d, mixed, gs, x2d, w_ssd, w_f, b_f, w_o, bm):
    t = x2d.shape[0]
    const = lambda r, c: pl.BlockSpec((r, c), lambda i: (0, 0))
    return pl.pallas_call(
        _merge_kernel,
        grid=(t // bm,),
        in_specs=[
            pl.BlockSpec((bm, D_INNER), lambda i: (i, 0)),
            pl.BlockSpec((bm, D_FOURIER), lambda i: (i, 0)),
            pl.BlockSpec((bm, 2 * D_MODEL), lambda i: (i, 0)),
            pl.BlockSpec((bm, D_MODEL), lambda i: (i, 0)),
            const(D_INNER, D_MODEL), const(D_FOURIER, D_MODEL), const(1, D_MODEL), const(D_MODEL, D_MODEL),
        ],
        out_specs=pl.BlockSpec((bm, D_MODEL), lambda i: (i, 0)),
        out_shape=jax.ShapeDtypeStruct((t, D_MODEL), F32),
        compiler_params=pltpu.CompilerParams(
            dimension_semantics=("arbitrary",), vmem_limit_bytes=VMEM_LIMIT),
        name="merge",
    )(y2d, mixed, gs, x2d, w_ssd, w_f, b_f, w_o)


def _ffn_kernel(x_ref, g_ref, wg_ref, wu_ref, wd_ref, gfin_ref, o_ref, *, ff_tile):
    x = x_ref[...]
    h = (_rms_scale(x) * g_ref[...]).astype(BF16)
    y = x
    for f in range(D_FF // ff_tile):
        cols = slice(f * ff_tile, (f + 1) * ff_tile)
        act = (_silu_of_twice(_dot(h, wg_ref[:, cols])) * _dot(h, wu_ref[:, cols])).astype(BF16)
        y = y + _dot(act, wd_ref[cols, :])
    o_ref[...] = _rms_scale(y) * gfin_ref[...]


def _ffn(x1, norm_g, w_gate, w_up, w_down, norm_fin, bm, ff_tile):
    t = x1.shape[0]
    resident = lambda a: pl.BlockSpec(a.shape, lambda i: (0, 0), pipeline_mode=pl.Buffered(1))
    return pl.pallas_call(
        functools.partial(_ffn_kernel, ff_tile=ff_tile),
        grid=(t // bm,),
        in_specs=[
            pl.BlockSpec((bm, D_MODEL), lambda i: (i, 0)),
            resident(norm_g), resident(w_gate), resident(w_up), resident(w_down), resident(norm_fin),
        ],
        out_specs=pl.BlockSpec((bm, D_MODEL), lambda i: (i, 0)),
        out_shape=jax.ShapeDtypeStruct((t, D_MODEL), F32),
        compiler_params=pltpu.CompilerParams(
            dimension_semantics=("arbitrary",), vmem_limit_bytes=VMEM_LIMIT),
        name="ffn",
    )(x1, norm_g, w_gate, w_up, w_down, norm_fin)


def _prep_weights(norm_mix, w_in, conv_w, conv_b, dt_bias_f, dt_bias_b, a_log_f, a_log_b, d_skip,
                  ssd_norm, w_ssd_out, w_fourier_out, b_fourier_out, w_out, norm_ffn, w_gate_up,
                  w_down, norm_final):
    o_dt = D_INNER + (D_INNER + 2 * SSD_GROUPS * D_STATE)
    o_u = o_dt + 2 * SSD_HEADS
    o_g = o_u + D_FOURIER
    w_all = jnp.concatenate([0.5 * w_in[:, :D_INNER], w_in[:, D_INNER:o_dt], 0.5 * w_in[:, o_g:],
                             w_in[:, o_u:o_g]], axis=1).astype(BF16)

    def by_group(f, b):
        return jnp.concatenate([f.reshape(SSD_GROUPS, HEADS_PER_GROUP),
                                b.reshape(SSD_GROUPS, HEADS_PER_GROUP)], axis=1).reshape(-1)

    w_dt_cols = w_in[:, o_dt:o_u]
    w_dt = jnp.concatenate([w_dt_cols[:, :SSD_HEADS].reshape(D_MODEL, SSD_GROUPS, HEADS_PER_GROUP),
                            w_dt_cols[:, SSD_HEADS:].reshape(D_MODEL, SSD_GROUPS, HEADS_PER_GROUP)],
                           axis=2).reshape(D_MODEL, 2 * SSD_HEADS)
    w_dt = jnp.pad(w_dt, ((0, 0), (0, LANES - 2 * SSD_HEADS))).astype(BF16)
    rep = lambda v: jnp.broadcast_to(v.astype(F32)[:, None], (2 * SSD_HEADS, LANES))
    return dict(
        norm_mix=norm_mix.reshape(1, D_MODEL), w_all=w_all, w_dt=w_dt,
        conv_w=0.5 * conv_w, conv_b=0.5 * conv_b.reshape(1, -1),
        bias_rep=rep(by_group(dt_bias_f, dt_bias_b)), alog_rep=rep(by_group(a_log_f, a_log_b)),
        dskip_exp=jnp.repeat(d_skip.astype(F32), HEAD_DIM).reshape(1, D_INNER),
        ssd_norm=ssd_norm.reshape(1, D_INNER),
        w_ssd=w_ssd_out.astype(BF16), w_f=w_fourier_out.astype(BF16),
        b_f=b_fourier_out.reshape(1, D_MODEL), w_o=w_out.astype(BF16),
        norm_ffn=norm_ffn.reshape(1, D_MODEL),
        w_gate=(0.5 * w_gate_up[:, :D_FF]).astype(BF16), w_up=w_gate_up[:, D_FF:].astype(BF16),
        w_down=w_down.astype(BF16), norm_final=norm_final.reshape(1, D_MODEL),
    )


def _trunk(x, p):
    bsz, seq, _ = x.shape
    t = bsz * seq
    bm = min(1024, t)
    x2d = x.reshape(t, D_MODEL)
    zs, xsc, bt, cc, gs, zr, zi, dtt = _inproj(x2d, p["norm_mix"], p["w_all"], p["w_dt"],
                                               _channel_dft_matrix(), p["conv_w"], p["conv_b"], bm, seq)
    y = _ssd(xsc, bt, cc, zs, dtt, p["bias_rep"], p["alog_rep"], p["dskip_exp"], p["ssd_norm"], bsz, seq)
    mixed = _dft(zr, zi, _dft_constants(seq), bsz, seq)
    x1 = _merge(y.reshape(t, D_INNER), mixed, gs, x2d, p["w_ssd"], p["w_f"], p["b_f"], p["w_o"],
                min(512, t))
    out = _ffn(x1, p["norm_ffn"], p["w_gate"], p["w_up"], p["w_down"], p["norm_final"], min(512, t),
               D_FF // 2)
    return out.reshape(bsz, seq, D_MODEL)


def kernel(x_prompt, x_sample, norm_mix, w_in, conv_w, conv_b, dt_bias_f, dt_bias_b, a_log_f, a_log_b,
           d_skip, ssd_norm, w_ssd_out, w_fourier_out, b_fourier_out, w_out, norm_ffn, w_gate_up, w_down,
           norm_final):
    p = _prep_weights(norm_mix[0], w_in[0], conv_w[0], conv_b[0], dt_bias_f[0], dt_bias_b[0],
                      a_log_f[0], a_log_b[0], d_skip[0], ssd_norm[0], w_ssd_out[0], w_fourier_out[0],
                      b_fourier_out[0], w_out[0], norm_ffn[0], w_gate_up[0], w_down[0], norm_final)
    return (_trunk(x_prompt, p), _trunk(x_sample, p))
```

```python
import functools
import math

import numpy as np
import jax
import jax.numpy as jnp
from jax import lax
from jax.experimental import pallas as pl
from jax.experimental.pallas import tpu as pltpu

F32 = jnp.float32
BF16 = jnp.bfloat16

D_MODEL = 1024
D_INNER = 2048
HEAD_DIM = 64
SSD_HEADS = 32
SSD_GROUPS = 8
HEADS_PER_GROUP = 4
D_STATE = 128
CONV_WIDTH = 7
CONV_PAD = CONV_WIDTH // 2
CHUNK = 128
GROUP_CH = D_INNER // SSD_GROUPS
D_FOURIER = 1024
FOURIER_GROUP_DIM = 128
FOURIER_GROUPS = 8
D_FF = 2816
EPS = 1e-5

LANES = 128
SUBLANES = 8
BF16_ROWS = 16
DFT_N1 = 16
DFT_COLS = 256
PROJ_TILE = 1024
N_MAIN = 8192
VMEM_LIMIT = 56 * 1024 * 1024


def _silu_of_twice(t):
    return t + t * jnp.tanh(t)


def _sigmoid_of_twice(t):
    return 0.5 + 0.5 * jnp.tanh(t)


def _softplus(v):
    return jnp.maximum(v, 0.0) + jnp.log1p(jnp.exp(-jnp.abs(v)))


def _rms_scale(v):
    return v * lax.rsqrt(jnp.mean(v * v, axis=-1, keepdims=True) + EPS)


def _dot(a, b):
    return jnp.dot(a, b, preferred_element_type=F32)


TILE_XS = 2
TILE_B = 4
TILE_C = 5
TILE_GATE = 6
TILE_FOURIER = 8
HALO = BF16_ROWS
PROJ_ROW_BLOCKS = 2
CONV_ROW_BLOCKS = 8


def _inproj_kernel(x_ref, xp_ref, xn_ref, g_ref, w_ref, wdt_ref, cs_ref, cw_ref, cb_ref,
                   zs_ref, xsc_ref, bt_ref, cc_ref, gs_ref, zr_ref, zi_ref, dtt_ref,
                   h_ref, slab_ref, *, tiles_per_seq):
    i = pl.program_id(0)
    j = pl.program_id(1)
    bm = x_ref.shape[0]
    main = pl.ds(HALO, bm)

    @pl.when(j == 0)
    def _():
        g = g_ref[...]
        h = (_rms_scale(x_ref[...]) * g).astype(BF16)
        h_ref[main, :] = h
        pos = i % tiles_per_seq
        hp = _rms_scale(xp_ref[0]) * g
        hn = _rms_scale(xn_ref[0]) * g
        h_ref[0:HALO, :] = jnp.where(pos == 0, 0.0, hp).astype(BF16)
        h_ref[HALO + bm:, :] = jnp.where(pos == tiles_per_seq - 1, 0.0, hn).astype(BF16)
        dtt_ref[...] = _dot(h, wdt_ref[...]).T

    @pl.when(j < TILE_XS)
    def _():
        zs_ref[...] = _silu_of_twice(_dot(h_ref[main, :], w_ref[...])).astype(BF16)

    def conv_tile(store):
        n_pb, n_rb = PROJ_ROW_BLOCKS, CONV_ROW_BLOCKS
        rb_rows = (bm + 2 * HALO) // n_pb
        ob_rows = bm // n_rb
        for q in range(PROJ_TILE // (2 * LANES)):
            for r in range(n_pb):
                rows = slice(r * rb_rows, (r + 1) * rb_rows)
                pe = _dot(h_ref[rows, :], w_ref[:, q * 2 * LANES:(q + 1) * 2 * LANES])
                slab_ref[2 * q, rows, :] = pe[:, :LANES]
                slab_ref[2 * q + 1, rows, :] = pe[:, LANES:]
            for s in range(2 * q, 2 * q + 2):
                sl = slice(s * LANES, (s + 1) * LANES)
                for r in range(n_rb):
                    acc = cb_ref[:, sl]
                    for k in range(CONV_WIDTH):
                        lo = HALO - CONV_PAD + k + r * ob_rows
                        acc = acc + cw_ref[k:k + 1, sl] * slab_ref[s, lo:lo + ob_rows, :]
                    store(s, r, _silu_of_twice(acc))

    def store_rows(dst_ref):
        def store(s, r, v):
            dst_ref[r * CHUNK:(r + 1) * CHUNK, s * LANES:(s + 1) * LANES] = v.astype(BF16)
        return store

    def store_transposed(s, r, v):
        bt_ref[s, r * CHUNK:(r + 1) * CHUNK, :] = v.T.astype(BF16)

    @pl.when((j >= TILE_XS) & (j < TILE_B))
    def _():
        conv_tile(store_rows(xsc_ref))

    @pl.when(j == TILE_B)
    def _():
        conv_tile(store_transposed)

    @pl.when(j == TILE_C)
    def _():
        conv_tile(store_rows(cc_ref))

    @pl.when((j >= TILE_GATE) & (j < TILE_FOURIER))
    def _():
        gs_ref[...] = _sigmoid_of_twice(_dot(h_ref[main, :], w_ref[...])).astype(BF16)

    @pl.when(j == TILE_FOURIER)
    def _():
        u = _dot(h_ref[main, :], w_ref[...]).astype(BF16)
        for g in range(FOURIER_GROUPS):
            sl = slice(g * FOURIER_GROUP_DIM, (g + 1) * FOURIER_GROUP_DIM)
            z = _dot(u[:, sl], cs_ref[...])
            zr_ref[:, sl] = z[:, :FOURIER_GROUP_DIM].astype(BF16)
            zi_ref[:, sl] = z[:, FOURIER_GROUP_DIM:].astype(BF16)


def _inproj(x2d, norm_g, w_all, w_dt, cs, conv_w, conv_b, bm, seq):
    t = x2d.shape[0]
    n_tiles = w_all.shape[1] // PROJ_TILE
    assert n_tiles == TILE_FOURIER + 1 and seq % bm == 0 and bm % HALO == 0
    assert bm == CONV_ROW_BLOCKS * CHUNK and (bm + 2 * HALO) % (PROJ_ROW_BLOCKS * SUBLANES) == 0
    halo_blocks = bm // HALO
    x_halo = x2d.reshape(t // HALO, HALO, D_MODEL)
    clip = lambda v, lo, hi: jnp.minimum(jnp.maximum(v, lo), hi)
    tok = lambda width, first, count: pl.BlockSpec(
        (bm, width), lambda i, j: (i, clip(j - first, 0, count - 1)))
    bf16_out = lambda cols: jax.ShapeDtypeStruct((t, cols), BF16)
    return pl.pallas_call(
        functools.partial(_inproj_kernel, tiles_per_seq=seq // bm),
        grid=(t // bm, n_tiles),
        in_specs=[
            pl.BlockSpec((bm, D_MODEL), lambda i, j: (i, 0)),
            pl.BlockSpec((1, HALO, D_MODEL), lambda i, j: (jnp.maximum(i * halo_blocks - 1, 0), 0, 0)),
            pl.BlockSpec((1, HALO, D_MODEL),
                         lambda i, j: (jnp.minimum((i + 1) * halo_blocks, t // HALO - 1), 0, 0)),
            pl.BlockSpec((1, D_MODEL), lambda i, j: (0, 0)),
            pl.BlockSpec((D_MODEL, PROJ_TILE), lambda i, j: (0, j)),
            pl.BlockSpec((D_MODEL, LANES), lambda i, j: (0, 0)),
            pl.BlockSpec((FOURIER_GROUP_DIM, 2 * FOURIER_GROUP_DIM), lambda i, j: (0, 0)),
            pl.BlockSpec((CONV_WIDTH, PROJ_TILE), lambda i, j: (0, clip(j - TILE_XS, 0, 3))),
            pl.BlockSpec((1, PROJ_TILE), lambda i, j: (0, clip(j - TILE_XS, 0, 3))),
        ],
        out_specs=[
            tok(PROJ_TILE, 0, 2),
            tok(PROJ_TILE, TILE_XS, 2),
            pl.BlockSpec((SSD_GROUPS, bm, D_STATE), lambda i, j: (0, i, 0)),
            tok(PROJ_TILE, TILE_C, 1),
            tok(PROJ_TILE, TILE_GATE, 2),
            tok(D_FOURIER, TILE_FOURIER, 1),
            tok(D_FOURIER, TILE_FOURIER, 1),
            pl.BlockSpec((LANES, bm), lambda i, j: (0, i)),
        ],
        out_shape=[
            bf16_out(D_INNER), bf16_out(D_INNER),
            jax.ShapeDtypeStruct((SSD_GROUPS, t, D_STATE), BF16),
            bf16_out(SSD_GROUPS * D_STATE), bf16_out(2 * D_MODEL),
            bf16_out(D_FOURIER), bf16_out(D_FOURIER),
            jax.ShapeDtypeStruct((LANES, t), F32),
        ],
        scratch_shapes=[
            pltpu.VMEM((bm + 2 * HALO, D_MODEL), BF16),
            pltpu.VMEM((PROJ_TILE // LANES, bm + 2 * HALO, LANES), F32),
        ],
        compiler_params=pltpu.CompilerParams(
            dimension_semantics=("arbitrary", "arbitrary"), vmem_limit_bytes=VMEM_LIMIT),
        name="inproj",
    )(x2d, x_halo, x_halo, norm_g, w_all, w_dt, cs, conv_w, conv_b)


Q_E = 48
N_PIECES = 3
HEAD_DIRS = 2 * HEADS_PER_GROUP
SSD_UNROLL = 8


def _split3(v):
    p0 = v.astype(BF16).astype(F32)
    r1 = v - p0
    p1 = r1.astype(BF16).astype(F32)
    return p0, p1, (r1 - p1).astype(BF16).astype(F32)


def _ssd_kernel(xsb_ref, bt_ref, cc_ref, zs_ref, dtt_ref,
                bias_ref, alog_ref, dsk_ref, nrm_ref, exp_ref,
                out_ref,
                y_ref, locf_ref, locb_ref, sinf_ref, sinb_ref,
                qt_ref, rows_ref, cdf_ref, cdb_ref, *, seq):
    n_chunks = seq // CHUNK

    def per_chunk_group(fn, width):
        def body(i, carry):
            for u in range(width):
                fn(width * i + u)
            return carry
        lax.fori_loop(0, n_chunks // width, body, 0)

    row_s = lax.broadcasted_iota(jnp.int32, (HEAD_DIRS, seq), 0)
    lane_s = lax.broadcasted_iota(jnp.int32, (HEAD_DIRS, seq), 1) & (CHUNK - 1)
    dt_all = _softplus(dtt_ref[...] + bias_ref[:, 0:1])
    dta = dt_all * (-jnp.exp(alog_ref[:, 0:1]))
    cf = dta
    rb = dta
    k = 1
    while k < CHUNK:
        cf = cf + jnp.where(lane_s >= k, pltpu.roll(cf, k, 1), 0.0)
        rb = rb + jnp.where(lane_s < CHUNK - k, pltpu.roll(rb, seq - k, 1), 0.0)
        k *= 2
    rows_ref[0] = dt_all
    rows_ref[1] = jnp.where(row_s < HEADS_PER_GROUP, cf, rb)

    dskip = dsk_ref[...]

    row8 = lax.broadcasted_iota(jnp.int32, (SUBLANES, LANES), 0)
    li = lax.broadcasted_iota(jnp.int32, (CHUNK, CHUNK), 0)
    si = lax.broadcasted_iota(jnp.int32, (CHUNK, CHUNK), 1)
    lane_g = lax.broadcasted_iota(jnp.int32, (CHUNK, GROUP_CH), 1)
    is_fwd_row = row8 < HEADS_PER_GROUP
    zero8 = jnp.zeros((SUBLANES, LANES), F32)
    lower_b = jnp.where(li >= si, 1.0, 0.0).astype(BF16)
    upper_b = jnp.where(li <= si, 1.0, 0.0).astype(BF16)

    def diag_pass(c):
        t0 = pl.multiple_of(c * CHUNK, CHUNK)
        cb = cc_ref[pl.ds(t0, CHUNK), :]
        bt = bt_ref[pl.ds(t0, CHUNK), :]
        xb = xsb_ref[pl.ds(t0, CHUNK), :]
        scores = _dot(cb, bt).astype(BF16)

        dt = rows_ref[0, :, pl.ds(t0, CHUNK)]
        cum = rows_ref[1, :, pl.ds(t0, CHUNK)]
        tot = jnp.where(is_fwd_row, cum[:, CHUNK - 1:CHUNK], cum[:, 0:1])
        wrow = dt * jnp.exp(tot - cum)
        ep = _split3(jnp.exp(cum))

        q = jnp.concatenate([zero8] * (Q_E // SUBLANES) + [ep[0], ep[1], ep[2]]
                            + [zero8] * ((CHUNK - Q_E) // SUBLANES - N_PIECES), axis=0)
        qtf = q.T
        qt_ref[c] = qtf.astype(BF16)

        edge = jnp.concatenate([qtf[0:SUBLANES, :], qtf[CHUNK - SUBLANES:, :]], axis=0).astype(BF16)
        cd = _dot(edge, exp_ref[...])
        cdf_ref[pl.ds(c, 1), :] = cd[2 * SUBLANES - 1:2 * SUBLANES, :GROUP_CH]
        cdb_ref[pl.ds(c, 1), :] = cd[0:1, GROUP_CH:]

        dtb = dt.astype(BF16)
        wb = wrow.astype(BF16)
        lhs_m = []
        lhs_f = []
        lhs_b = []
        xm = []
        for h in range(HEADS_PER_GROUP):
            hb = HEADS_PER_GROUP + h
            rf = jnp.broadcast_to(cum[h:h + 1, :], (CHUNK, CHUNK))
            rbk = jnp.broadcast_to(cum[hb:hb + 1, :], (CHUNK, CHUNK))
            arg = jnp.where(li >= si, rf.T - rf, rbk.T - rbk)
            coef = lower_b * dtb[h:h + 1, :] + upper_b * dtb[hb:hb + 1, :]
            lhs_m.append(scores * jnp.exp(arg).astype(BF16) * coef)
            lhs_f.append(bt * wb[h:h + 1, :])
            lhs_b.append(bt * wb[hb:hb + 1, :])
            in_head = (lane_g >= h * HEAD_DIM) & (lane_g < (h + 1) * HEAD_DIM)
            xm.append(jnp.where(in_head, xb, jnp.zeros_like(xb)))
        lhs = jnp.concatenate([jnp.concatenate(lhs_m, axis=1),
                               jnp.concatenate(lhs_f, axis=1),
                               jnp.concatenate(lhs_b, axis=1)], axis=0)
        big = _dot(lhs, jnp.concatenate(xm, axis=0))
        y_ref[pl.ds(t0, CHUNK), :] = big[:CHUNK] + dskip * xb.astype(F32)
        locf_ref[c] = big[CHUNK:2 * CHUNK]
        locb_ref[c] = big[2 * CHUNK:]

    per_chunk_group(diag_pass, 2 * SSD_UNROLL)

    def scan_fwd(i, state):
        sinf_ref[i] = state.astype(BF16)
        return state * cdf_ref[pl.ds(i, 1), :] + locf_ref[i]

    def scan_bwd(i, state):
        j = n_chunks - 1 - i
        sinb_ref[j] = state.astype(BF16)
        return state * cdb_ref[pl.ds(j, 1), :] + locb_ref[j]

    lax.fori_loop(0, n_chunks, scan_fwd, jnp.zeros((D_STATE, GROUP_CH), F32))
    lax.fori_loop(0, n_chunks, scan_bwd, jnp.zeros((D_STATE, GROUP_CH), F32))

    nrm = nrm_ref[...]

    def final_pass(c):
        t0 = pl.multiple_of(c * CHUNK, CHUNK)
        st = jnp.concatenate([sinf_ref[c], sinb_ref[c]], axis=1)
        yo = _dot(cc_ref[pl.ds(t0, CHUNK), :], st)
        ee = _dot(qt_ref[c], exp_ref[...])
        y = (y_ref[pl.ds(t0, CHUNK), :] + yo[:, :GROUP_CH] * ee[:, :GROUP_CH]
             + yo[:, GROUP_CH:] * ee[:, GROUP_CH:])
        y = y * zs_ref[pl.ds(t0, CHUNK), :].astype(F32)
        out_ref[pl.ds(t0, CHUNK), :] = (_rms_scale(y) * nrm).astype(BF16)

    per_chunk_group(final_pass, 2 * SSD_UNROLL)


def _expand_matrix():
    expand = np.zeros((CHUNK, 2 * GROUP_CH), np.float32)
    for r in range(N_PIECES):
        for j in range(HEAD_DIRS):
            expand[Q_E + SUBLANES * r + j, j * HEAD_DIM:(j + 1) * HEAD_DIM] = 1.0
    return jnp.asarray(expand).astype(BF16)


def _ssd(xsc, bt, cc, zs, dtt, bias_rep, alog_rep, dskip_exp, ssd_norm, bsz, seq):
    n_chunks = seq // CHUNK
    assert n_chunks % (2 * SSD_UNROLL) == 0
    expand = _expand_matrix()
    row_spec = lambda width: pl.BlockSpec((None, seq, width), lambda b, g: (b, 0, g))
    par_spec = lambda rows, width: pl.BlockSpec((rows, width), lambda b, g: (0, g))
    return pl.pallas_call(
        functools.partial(_ssd_kernel, seq=seq),
        grid=(bsz, SSD_GROUPS),
        in_specs=[
            row_spec(GROUP_CH),
            pl.BlockSpec((None, None, seq, D_STATE), lambda b, g: (g, b, 0, 0)),
            row_spec(D_STATE), row_spec(GROUP_CH),
            pl.BlockSpec((HEAD_DIRS, seq), lambda b, g: (g, b)),
            pl.BlockSpec((HEAD_DIRS, LANES), lambda b, g: (g, 0)),
            pl.BlockSpec((HEAD_DIRS, LANES), lambda b, g: (g, 0)),
            par_spec(1, GROUP_CH), par_spec(1, GROUP_CH),
            pl.BlockSpec(expand.shape, lambda b, g: (0, 0)),
        ],
        out_specs=pl.BlockSpec((None, seq, GROUP_CH), lambda b, g: (b, 0, g)),
        out_shape=jax.ShapeDtypeStruct((bsz, seq, D_INNER), BF16),
        scratch_shapes=[
            pltpu.VMEM((seq, GROUP_CH), F32),
            pltpu.VMEM((n_chunks, D_STATE, GROUP_CH), F32),
            pltpu.VMEM((n_chunks, D_STATE, GROUP_CH), F32),
            pltpu.VMEM((n_chunks, D_STATE, GROUP_CH), BF16),
            pltpu.VMEM((n_chunks, D_STATE, GROUP_CH), BF16),
            pltpu.VMEM((n_chunks, CHUNK, CHUNK), BF16),
            pltpu.VMEM((2, HEAD_DIRS, seq), F32),
            pltpu.VMEM((n_chunks, GROUP_CH), F32),
            pltpu.VMEM((n_chunks, GROUP_CH), F32),
        ],
        compiler_params=pltpu.CompilerParams(
            dimension_semantics=("arbitrary", "arbitrary"), vmem_limit_bytes=VMEM_LIMIT),
        name="ssd",
    )(xsc.reshape(bsz, seq, D_INNER), bt.reshape(SSD_GROUPS, bsz, seq, D_STATE),
      cc.reshape(bsz, seq, SSD_GROUPS * D_STATE), zs.reshape(bsz, seq, D_INNER), dtt,
      bias_rep, alog_rep, dskip_exp, ssd_norm, expand)


def _dft_kernel(zr_ref, zi_ref, a1_ref, a2_ref, a3_ref, o_ref, sr_ref, si_ref, *, n2):
    cols = zr_ref.shape[-1]
    rows1 = DFT_N1 * BF16_ROWS
    rows2 = n2 * BF16_ROWS

    def stack(r_ref, i_ref, idx, rows):
        return jnp.concatenate([r_ref[idx].reshape(rows, cols), i_ref[idx].reshape(rows, cols)], axis=0)

    for t2 in range(n2):
        y = _dot(a1_ref[...], stack(zr_ref, zi_ref, (slice(None), t2), rows1))
        sr_ref[:, t2] = y[:rows1].astype(BF16).reshape(DFT_N1, BF16_ROWS, cols)
        si_ref[:, t2] = y[rows1:].astype(BF16).reshape(DFT_N1, BF16_ROWS, cols)
    for t3 in range(BF16_ROWS):
        y = _dot(a2_ref[...], stack(sr_ref, si_ref, t3, rows2))
        sr_ref[t3] = y[:rows2].astype(BF16).reshape(n2, BF16_ROWS, cols)
        si_ref[t3] = y[rows2:].astype(BF16).reshape(n2, BF16_ROWS, cols)
    for k2 in range(n2):
        y = _dot(a3_ref[k2], stack(sr_ref, si_ref, (slice(None), k2), rows1))
        o_ref[:, k2] = y.astype(BF16).reshape(BF16_ROWS, BF16_ROWS, cols)


def _dft(zr, zi, consts, bsz, seq):
    a1, a2, a3 = consts
    n2 = seq // (DFT_N1 * BF16_ROWS)
    shape5 = (bsz, DFT_N1, n2, BF16_ROWS, D_FOURIER)
    spec = pl.BlockSpec((None, DFT_N1, n2, BF16_ROWS, DFT_COLS), lambda b, j: (b, 0, 0, 0, j))
    whole = lambda a: pl.BlockSpec(a.shape, lambda b, j: (0,) * a.ndim)
    out = pl.pallas_call(
        functools.partial(_dft_kernel, n2=n2),
        grid=(bsz, D_FOURIER // DFT_COLS),
        in_specs=[spec, spec, whole(a1), whole(a2), whole(a3)],
        out_specs=spec,
        out_shape=jax.ShapeDtypeStruct(shape5, BF16),
        scratch_shapes=[pltpu.VMEM((BF16_ROWS, n2, BF16_ROWS, DFT_COLS), BF16),
                        pltpu.VMEM((BF16_ROWS, n2, BF16_ROWS, DFT_COLS), BF16)],
        compiler_params=pltpu.CompilerParams(
            dimension_semantics=("arbitrary", "arbitrary"), vmem_limit_bytes=VMEM_LIMIT),
        name="dft",
    )(zr.reshape(shape5), zi.reshape(shape5), a1, a2, a3)
    return out.reshape(bsz * seq, D_FOURIER)


def _channel_dft_matrix():
    cidx = np.arange(FOURIER_GROUP_DIM)
    ang = 2.0 * np.pi * np.outer(cidx, cidx) / FOURIER_GROUP_DIM
    cs = np.concatenate([np.cos(ang), -np.sin(ang)], axis=1)
    return jnp.asarray(cs.astype(np.float32)).astype(BF16)


def _dft_constants(seq):
    n1, n3 = DFT_N1, BF16_ROWS
    n2 = seq // (n1 * n3)
    w = lambda n, e: np.exp(-2j * np.pi * (np.asarray(e) % n) / n)
    i16 = np.arange(n3)
    eye = np.eye(n3)
    m1 = np.einsum("kt,ab->aktb", w(n1, np.outer(np.arange(n1), np.arange(n1))), eye).reshape(n3 * n1, n1 * n3)
    k2 = np.arange(n2)
    f2 = w(n2, np.outer(k2, k2))[:, :, None] * w(n1 * n2, np.outer(k2, np.arange(n1)))[None, :, :]
    m2 = np.einsum("ktc,cd->kctd", f2, np.eye(n1)).reshape(n2 * n1, n2 * n1)
    f3 = (w(n3, np.outer(i16, i16))[None, :, :, None]
          * w(n2 * n3, np.outer(k2, i16))[:, None, :, None]
          * w(seq, np.outer(i16, np.arange(n1)))[None, None, :, :])
    m3 = np.einsum("jktc,cd->jkctd", f3, np.eye(n1)).reshape(n2, n3 * n1, n3 * n1)
    m3 = m3 / math.sqrt(seq * FOURIER_GROUP_DIM)
    full = lambda m: np.concatenate([np.concatenate([m.real, -m.imag], axis=-1),
                                     np.concatenate([m.imag, m.real], axis=-1)], axis=-2)
    real_part = lambda m: np.concatenate([m.real, -m.imag], axis=-1)
    to_bf16 = lambda a: jnp.asarray(a.astype(np.float32)).astype(BF16)
    return to_bf16(full(m1)), to_bf16(full(m2)), to_bf16(real_part(m3))


def _merge_kernel(y_ref, mx_ref, gs_ref, x_ref, wssd_ref, wf_ref, bf_ref, wo_ref, o_ref):
    a_out = _dot(y_ref[...], wssd_ref[...])
    f_out = _dot(mx_ref[...], wf_ref[...]) + bf_ref[...]
    gates = gs_ref[...].astype(F32)
    merged = (gates[:, :D_MODEL] * a_out + gates[:, D_MODEL:] * f_out).astype(BF16)
    o_ref[...] = x_ref[...] + _dot(merged, wo_ref[...])


def _merge(y2d, mixed, gs, x2d, w_ssd, w_f, b_f, w_o, bm):
    t = x2d.shape[0]
    const = lambda r, c: pl.BlockSpec((r, c), lambda i: (0, 0))
    return pl.pallas_call(
        _merge_kernel,
        grid=(t // bm,),
        in_specs=[
            pl.BlockSpec((bm, D_INNER), lambda i: (i, 0)),
            pl.BlockSpec((bm, D_FOURIER), lambda i: (i, 0)),
            pl.BlockSpec((bm, 2 * D_MODEL), lambda i: (i, 0)),
            pl.BlockSpec((bm, D_MODEL), lambda i: (i, 0)),
            const(D_INNER, D_MODEL), const(D_FOURIER, D_MODEL), const(1, D_MODEL), const(D_MODEL, D_MODEL),
        ],
        out_specs=pl.BlockSpec((bm, D_MODEL), lambda i: (i, 0)),
        out_shape=jax.ShapeDtypeStruct((t, D_MODEL), F32),
        compiler_params=pltpu.CompilerParams(
            dimension_semantics=("arbitrary",), vmem_limit_bytes=VMEM_LIMIT),
        name="merge",
    )(y2d, mixed, gs, x2d, w_ssd, w_f, b_f, w_o)


def _ffn_kernel(x_ref, g_ref, wg_ref, wu_ref, wd_ref, gfin_ref, o_ref, *, ff_tile):
    x = x_ref[...]
    h = (_rms_scale(x) * g_ref[...]).astype(BF16)
    y = x
    for f in range(D_FF // ff_tile):
        cols = slice(f * ff_tile, (f + 1) * ff_tile)
        act = (_silu_of_twice(_dot(h, wg_ref[:, cols])) * _dot(h, wu_ref[:, cols])).astype(BF16)
        y = y + _dot(act, wd_ref[cols, :])
    o_ref[...] = _rms_scale(y) * gfin_ref[...]


def _ffn(x1, norm_g, w_gate, w_up, w_down, norm_fin, bm, ff_tile):
    t = x1.shape[0]
    resident = lambda a: pl.BlockSpec(a.shape, lambda i: (0, 0), pipeline_mode=pl.Buffered(1))
    return pl.pallas_call(
        functools.partial(_ffn_kernel, ff_tile=ff_tile),
        grid=(t // bm,),
        in_specs=[
            pl.BlockSpec((bm, D_MODEL), lambda i: (i, 0)),
            resident(norm_g), resident(w_gate), resident(w_up), resident(w_down), resident(norm_fin),
        ],
        out_specs=pl.BlockSpec((bm, D_MODEL), lambda i: (i, 0)),
        out_shape=jax.ShapeDtypeStruct((t, D_MODEL), F32),
        compiler_params=pltpu.CompilerParams(
            dimension_semantics=("arbitrary",), vmem_limit_bytes=VMEM_LIMIT),
        name="ffn",
    )(x1, norm_g, w_gate, w_up, w_down, norm_fin)


def _prep_weights(norm_mix, w_in, conv_w, conv_b, dt_bias_f, dt_bias_b, a_log_f, a_log_b, d_skip,
                  ssd_norm, w_ssd_out, w_fourier_out, b_fourier_out, w_out, norm_ffn, w_gate_up,
                  w_down, norm_final):
    o_dt = D_INNER + (D_INNER + 2 * SSD_GROUPS * D_STATE)
    o_u = o_dt + 2 * SSD_HEADS
    o_g = o_u + D_FOURIER
    w_all = jnp.concatenate([0.5 * w_in[:, :D_INNER], w_in[:, D_INNER:o_dt], 0.5 * w_in[:, o_g:],
                             w_in[:, o_u:o_g]], axis=1).astype(BF16)

    def by_group(f, b):
        return jnp.concatenate([f.reshape(SSD_GROUPS, HEADS_PER_GROUP),
                                b.reshape(SSD_GROUPS, HEADS_PER_GROUP)], axis=1).reshape(-1)

    w_dt_cols = w_in[:, o_dt:o_u]
    w_dt = jnp.concatenate([w_dt_cols[:, :SSD_HEADS].reshape(D_MODEL, SSD_GROUPS, HEADS_PER_GROUP),
                            w_dt_cols[:, SSD_HEADS:].reshape(D_MODEL, SSD_GROUPS, HEADS_PER_GROUP)],
                           axis=2).reshape(D_MODEL, 2 * SSD_HEADS)
    w_dt = jnp.pad(w_dt, ((0, 0), (0, LANES - 2 * SSD_HEADS))).astype(BF16)
    rep = lambda v: jnp.broadcast_to(v.astype(F32)[:, None], (2 * SSD_HEADS, LANES))
    return dict(
        norm_mix=norm_mix.reshape(1, D_MODEL), w_all=w_all, w_dt=w_dt,
        conv_w=0.5 * conv_w, conv_b=0.5 * conv_b.reshape(1, -1),
        bias_rep=rep(by_group(dt_bias_f, dt_bias_b)), alog_rep=rep(by_group(a_log_f, a_log_b)),
        dskip_exp=jnp.repeat(d_skip.astype(F32), HEAD_DIM).reshape(1, D_INNER),
        ssd_norm=ssd_norm.reshape(1, D_INNER),
        w_ssd=w_ssd_out.astype(BF16), w_f=w_fourier_out.astype(BF16),
        b_f=b_fourier_out.reshape(1, D_MODEL), w_o=w_out.astype(BF16),
        norm_ffn=norm_ffn.reshape(1, D_MODEL),
        w_gate=(0.5 * w_gate_up[:, :D_FF]).astype(BF16), w_up=w_gate_up[:, D_FF:].astype(BF16),
        w_down=w_down.astype(BF16), norm_final=norm_final.reshape(1, D_MODEL),
    )


def _trunk(x, p):
    bsz, seq, _ = x.shape
    t = bsz * seq
    bm = min(1024, t)
    x2d = x.reshape(t, D_MODEL)
    zs, xsc, bt, cc, gs, zr, zi, dtt = _inproj(x2d, p["norm_mix"], p["w_all"], p["w_dt"],
                                               _channel_dft_matrix(), p["conv_w"], p["conv_b"], bm, seq)
    y = _ssd(xsc, bt, cc, zs, dtt, p["bias_rep"], p["alog_rep"], p["dskip_exp"], p["ssd_norm"], bsz, seq)
    mixed = _dft(zr, zi, _dft_constants(seq), bsz, seq)
    x1 = _merge(y.reshape(t, D_INNER), mixed, gs, x2d, p["w_ssd"], p["w_f"], p["b_f"], p["w_o"],
                min(512, t))
    out = _ffn(x1, p["norm_ffn"], p["w_gate"], p["w_up"], p["w_down"], p["norm_final"], min(512, t),
               D_FF // 2)
    return out.reshape(bsz, seq, D_MODEL)


def kernel(x_prompt, x_sample, norm_mix, w_in, conv_w, conv_b, dt_bias_f, dt_bias_b, a_log_f, a_log_b,
           d_skip, ssd_norm, w_ssd_out, w_fourier_out, b_fourier_out, w_out, norm_ffn, w_gate_up, w_down,
           norm_final):
    p = _prep_weights(norm_mix[0], w_in[0], conv_w[0], conv_b[0], dt_bias_f[0], dt_bias_b[0],
                      a_log_f[0], a_log_b[0], d_skip[0], ssd_norm[0], w_ssd_out[0], w_fourier_out[0],
                      b_fourier_out[0], w_out[0], norm_ffn[0], w_gate_up[0], w_down[0], norm_final)
    return (_trunk(x_prompt, p), _trunk(x_sample, p))
```

```python
import functools
import math

import numpy as np
import jax
import jax.numpy as jnp
from jax import lax
from jax.experimental import pallas as pl
from jax.experimental.pallas import tpu as pltpu

F32 = jnp.float32
BF16 = jnp.bfloat16

D_MODEL = 1024
D_INNER = 2048
HEAD_DIM = 64
SSD_HEADS = 32
SSD_GROUPS = 8
HEADS_PER_GROUP = 4
D_STATE = 128
CONV_WIDTH = 7
CONV_PAD = CONV_WIDTH // 2
CHUNK = 128
GROUP_CH = D_INNER // SSD_GROUPS
D_FOURIER = 1024
FOURIER_GROUP_DIM = 128
FOURIER_GROUPS = 8
D_FF = 2816
EPS = 1e-5

LANES = 128
SUBLANES = 8
BF16_ROWS = 16
DFT_N1 = 16
DFT_COLS = 256
PROJ_TILE = 1024
N_MAIN = 8192
VMEM_LIMIT = 56 * 1024 * 1024


def _silu_of_twice(t):
    return t + t * jnp.tanh(t)


def _sigmoid_of_twice(t):
    return 0.5 + 0.5 * jnp.tanh(t)


def _softplus(v):
    return jnp.maximum(v, 0.0) + jnp.log1p(jnp.exp(-jnp.abs(v)))


def _rms_scale(v):
    return v * lax.rsqrt(jnp.mean(v * v, axis=-1, keepdims=True) + EPS)


def _dot(a, b):
    return jnp.dot(a, b, preferred_element_type=F32)


TILE_XS = 2
TILE_B = 4
TILE_C = 5
TILE_GATE = 6
TILE_FOURIER = 8
HALO = BF16_ROWS
PROJ_ROW_BLOCKS = 2
CONV_ROW_BLOCKS = 8


def _inproj_kernel(x_ref, xp_ref, xn_ref, g_ref, w_ref, wdt_ref, cs_ref, cw_ref, cb_ref,
                   zs_ref, xsc_ref, bt_ref, cc_ref, gs_ref, zr_ref, zi_ref, dtt_ref,
                   h_ref, slab_ref, *, tiles_per_seq):
    i = pl.program_id(0)
    j = pl.program_id(1)
    bm = x_ref.shape[0]
    main = pl.ds(HALO, bm)

    @pl.when(j == 0)
    def _():
        g = g_ref[...]
        h = (_rms_scale(x_ref[...]) * g).astype(BF16)
        h_ref[main, :] = h
        pos = i % tiles_per_seq
        hp = _rms_scale(xp_ref[0]) * g
        hn = _rms_scale(xn_ref[0]) * g
        h_ref[0:HALO, :] = jnp.where(pos == 0, 0.0, hp).astype(BF16)
        h_ref[HALO + bm:, :] = jnp.where(pos == tiles_per_seq - 1, 0.0, hn).astype(BF16)
        dtt_ref[...] = _dot(h, wdt_ref[...]).T

    @pl.when(j < TILE_XS)
    def _():
        zs_ref[...] = _silu_of_twice(_dot(h_ref[main, :], w_ref[...])).astype(BF16)

    def conv_tile(store):
        n_pb, n_rb = PROJ_ROW_BLOCKS, CONV_ROW_BLOCKS
        rb_rows = (bm + 2 * HALO) // n_pb
        ob_rows = bm // n_rb
        for q in range(PROJ_TILE // (2 * LANES)):
            for r in range(n_pb):
                rows = slice(r * rb_rows, (r + 1) * rb_rows)
                pe = _dot(h_ref[rows, :], w_ref[:, q * 2 * LANES:(q + 1) * 2 * LANES])
                slab_ref[2 * q, rows, :] = pe[:, :LANES]
                slab_ref[2 * q + 1, rows, :] = pe[:, LANES:]
            for s in range(2 * q, 2 * q + 2):
                sl = slice(s * LANES, (s + 1) * LANES)
                for r in range(n_rb):
                    acc = cb_ref[:, sl]
                    for k in range(CONV_WIDTH):
                        lo = HALO - CONV_PAD + k + r * ob_rows
                        acc = acc + cw_ref[k:k + 1, sl] * slab_ref[s, lo:lo + ob_rows, :]
                    store(s, r, _silu_of_twice(acc))

    def store_rows(dst_ref):
        def store(s, r, v):
            dst_ref[r * CHUNK:(r + 1) * CHUNK, s * LANES:(s + 1) * LANES] = v.astype(BF16)
        return store

    def store_transposed(s, r, v):
        bt_ref[s, r * CHUNK:(r + 1) * CHUNK, :] = v.T.astype(BF16)

    @pl.when((j >= TILE_XS) & (j < TILE_B))
    def _():
        conv_tile(store_rows(xsc_ref))

    @pl.when(j == TILE_B)
    def _():
        conv_tile(store_transposed)

    @pl.when(j == TILE_C)
    def _():
        conv_tile(store_rows(cc_ref))

    @pl.when((j >= TILE_GATE) & (j < TILE_FOURIER))
    def _():
        gs_ref[...] = _sigmoid_of_twice(_dot(h_ref[main, :], w_ref[...])).astype(BF16)

    @pl.when(j == TILE_FOURIER)
    def _():
        u = _dot(h_ref[main, :], w_ref[...]).astype(BF16)
        for g in range(FOURIER_GROUPS):
            sl = slice(g * FOURIER_GROUP_DIM, (g + 1) * FOURIER_GROUP_DIM)
            z = _dot(u[:, sl], cs_ref[...])
            zr_ref[:, sl] = z[:, :FOURIER_GROUP_DIM].astype(BF16)
            zi_ref[:, sl] = z[:, FOURIER_GROUP_DIM:].astype(BF16)


def _inproj(x2d, norm_g, w_all, w_dt, cs, conv_w, conv_b, bm, seq):
    t = x2d.shape[0]
    n_tiles = w_all.shape[1] // PROJ_TILE
    assert n_tiles == TILE_FOURIER + 1 and seq % bm == 0 and bm % HALO == 0
    assert bm == CONV_ROW_BLOCKS * CHUNK and (bm + 2 * HALO) % (PROJ_ROW_BLOCKS * SUBLANES) == 0
    halo_blocks = bm // HALO
    x_halo = x2d.reshape(t // HALO, HALO, D_MODEL)
    clip = lambda v, lo, hi: jnp.minimum(jnp.maximum(v, lo), hi)
    tok = lambda width, first, count: pl.BlockSpec(
        (bm, width), lambda i, j: (i, clip(j - first, 0, count - 1)))
    bf16_out = lambda cols: jax.ShapeDtypeStruct((t, cols), BF16)
    return pl.pallas_call(
        functools.partial(_inproj_kernel, tiles_per_seq=seq // bm),
        grid=(t // bm, n_tiles),
        in_specs=[
            pl.BlockSpec((bm, D_MODEL), lambda i, j: (i, 0)),
            pl.BlockSpec((1, HALO, D_MODEL), lambda i, j: (jnp.maximum(i * halo_blocks - 1, 0), 0, 0)),
            pl.BlockSpec((1, HALO, D_MODEL),
                         lambda i, j: (jnp.minimum((i + 1) * halo_blocks, t // HALO - 1), 0, 0)),
            pl.BlockSpec((1, D_MODEL), lambda i, j: (0, 0)),
            pl.BlockSpec((D_MODEL, PROJ_TILE), lambda i, j: (0, j)),
            pl.BlockSpec((D_MODEL, LANES), lambda i, j: (0, 0)),
            pl.BlockSpec((FOURIER_GROUP_DIM, 2 * FOURIER_GROUP_DIM), lambda i, j: (0, 0)),
            pl.BlockSpec((CONV_WIDTH, PROJ_TILE), lambda i, j: (0, clip(j - TILE_XS, 0, 3))),
            pl.BlockSpec((1, PROJ_TILE), lambda i, j: (0, clip(j - TILE_XS, 0, 3))),
        ],
        out_specs=[
            tok(PROJ_TILE, 0, 2),
            tok(PROJ_TILE, TILE_XS, 2),
            pl.BlockSpec((SSD_GROUPS, bm, D_STATE), lambda i, j: (0, i, 0)),
            tok(PROJ_TILE, TILE_C, 1),
            tok(PROJ_TILE, TILE_GATE, 2),
            tok(D_FOURIER, TILE_FOURIER, 1),
            tok(D_FOURIER, TILE_FOURIER, 1),
            pl.BlockSpec((LANES, bm), lambda i, j: (0, i)),
        ],
        out_shape=[
            bf16_out(D_INNER), bf16_out(D_INNER),
            jax.ShapeDtypeStruct((SSD_GROUPS, t, D_STATE), BF16),
            bf16_out(SSD_GROUPS * D_STATE), bf16_out(2 * D_MODEL),
            bf16_out(D_FOURIER), bf16_out(D_FOURIER),
            jax.ShapeDtypeStruct((LANES, t), F32),
        ],
        scratch_shapes=[
            pltpu.VMEM((bm + 2 * HALO, D_MODEL), BF16),
            pltpu.VMEM((PROJ_TILE // LANES, bm + 2 * HALO, LANES), F32),
        ],
        compiler_params=pltpu.CompilerParams(
            dimension_semantics=("arbitrary", "arbitrary"), vmem_limit_bytes=VMEM_LIMIT),
        name="inproj",
    )(x2d, x_halo, x_halo, norm_g, w_all, w_dt, cs, conv_w, conv_b)


Q_E = 48
N_PIECES = 3
HEAD_DIRS = 2 * HEADS_PER_GROUP
SSD_UNROLL = 8


def _split3(v):
    p0 = v.astype(BF16).astype(F32)
    r1 = v - p0
    p1 = r1.astype(BF16).astype(F32)
    return p0, p1, (r1 - p1).astype(BF16).astype(F32)


def _ssd_kernel(xsb_ref, bt_ref, cc_ref, zs_ref, dtt_ref,
                bias_ref, alog_ref, dsk_ref, nrm_ref, exp_ref,
                out_ref,
                y_ref, locf_ref, locb_ref, sinf_ref, sinb_ref,
                qt_ref, rows_ref, cdf_ref, cdb_ref, *, seq):
    n_chunks = seq // CHUNK

    def per_chunk_group(fn, width):
        def body(i, carry):
            for u in range(width):
                fn(width * i + u)
            return carry
        lax.fori_loop(0, n_chunks // width, body, 0)

    row_s = lax.broadcasted_iota(jnp.int32, (HEAD_DIRS, seq), 0)
    lane_s = lax.broadcasted_iota(jnp.int32, (HEAD_DIRS, seq), 1) & (CHUNK - 1)
    dt_all = _softplus(dtt_ref[...] + bias_ref[:, 0:1])
    dta = dt_all * (-jnp.exp(alog_ref[:, 0:1]))
    cf = dta
    rb = dta
    k = 1
    while k < CHUNK:
        cf = cf + jnp.where(lane_s >= k, pltpu.roll(cf, k, 1), 0.0)
        rb = rb + jnp.where(lane_s < CHUNK - k, pltpu.roll(rb, seq - k, 1), 0.0)
        k *= 2
    rows_ref[0] = dt_all
    rows_ref[1] = jnp.where(row_s < HEADS_PER_GROUP, cf, rb)

    dskip = dsk_ref[...]

    row8 = lax.broadcasted_iota(jnp.int32, (SUBLANES, LANES), 0)
    li = lax.broadcasted_iota(jnp.int32, (CHUNK, CHUNK), 0)
    si = lax.broadcasted_iota(jnp.int32, (CHUNK, CHUNK), 1)
    lane_g = lax.broadcasted_iota(jnp.int32, (CHUNK, GROUP_CH), 1)
    is_fwd_row = row8 < HEADS_PER_GROUP
    zero8 = jnp.zeros((SUBLANES, LANES), F32)
    lower_b = jnp.where(li >= si, 1.0, 0.0).astype(BF16)
    upper_b = jnp.where(li <= si, 1.0, 0.0).astype(BF16)

    def diag_pass(c):
        t0 = pl.multiple_of(c * CHUNK, CHUNK)
        cb = cc_ref[pl.ds(t0, CHUNK), :]
        bt = bt_ref[pl.ds(t0, CHUNK), :]
        xb = xsb_ref[pl.ds(t0, CHUNK), :]
        scores = _dot(cb, bt).astype(BF16)

        dt = rows_ref[0, :, pl.ds(t0, CHUNK)]
        cum = rows_ref[1, :, pl.ds(t0, CHUNK)]
        tot = jnp.where(is_fwd_row, cum[:, CHUNK - 1:CHUNK], cum[:, 0:1])
        wrow = dt * jnp.exp(tot - cum)
        ep = _split3(jnp.exp(cum))

        q = jnp.concatenate([zero8] * (Q_E // SUBLANES) + [ep[0], ep[1], ep[2]]
                            + [zero8] * ((CHUNK - Q_E) // SUBLANES - N_PIECES), axis=0)
        qtf = q.T
        qt_ref[c] = qtf.astype(BF16)

        edge = jnp.concatenate([qtf[0:SUBLANES, :], qtf[CHUNK - SUBLANES:, :]], axis=0).astype(BF16)
        cd = _dot(edge, exp_ref[...])
        cdf_ref[pl.ds(c, 1), :] = cd[2 * SUBLANES - 1:2 * SUBLANES, :GROUP_CH]
        cdb_ref[pl.ds(c, 1), :] = cd[0:1, GROUP_CH:]

        dtb = dt.astype(BF16)
        wb = wrow.astype(BF16)
        lhs_m = []
        lhs_f = []
        lhs_b = []
        xm = []
        for h in range(HEADS_PER_GROUP):
            hb = HEADS_PER_GROUP + h
            rf = jnp.broadcast_to(cum[h:h + 1, :], (CHUNK, CHUNK))
            rbk = jnp.broadcast_to(cum[hb:hb + 1, :], (CHUNK, CHUNK))
            arg = jnp.where(li >= si, rf.T - rf, rbk.T - rbk)
            coef = lower_b * dtb[h:h + 1, :] + upper_b * dtb[hb:hb + 1, :]
            lhs_m.append(scores * jnp.exp(arg).astype(BF16) * coef)
            lhs_f.append(bt * wb[h:h + 1, :])
            lhs_b.append(bt * wb[hb:hb + 1, :])
            in_head = (lane_g >= h * HEAD_DIM) & (lane_g < (h + 1) * HEAD_DIM)
            xm.append(jnp.where(in_head, xb, jnp.zeros_like(xb)))
        lhs = jnp.concatenate([jnp.concatenate(lhs_m, axis=1),
                               jnp.concatenate(lhs_f, axis=1),
                               jnp.concatenate(lhs_b, axis=1)], axis=0)
        big = _dot(lhs, jnp.concatenate(xm, axis=0))
        y_ref[pl.ds(t0, CHUNK), :] = big[:CHUNK] + dskip * xb.astype(F32)
        locf_ref[c] = big[CHUNK:2 * CHUNK]
        locb_ref[c] = big[2 * CHUNK:]

    per_chunk_group(diag_pass, 2 * SSD_UNROLL)

    def scan_fwd(i, state):
        sinf_ref[i] = state.astype(BF16)
        return state * cdf_ref[pl.ds(i, 1), :] + locf_ref[i]

    def scan_bwd(i, state):
        j = n_chunks - 1 - i
        sinb_ref[j] = state.astype(BF16)
        return state * cdb_ref[pl.ds(j, 1), :] + locb_ref[j]

    lax.fori_loop(0, n_chunks, scan_fwd, jnp.zeros((D_STATE, GROUP_CH), F32))
    lax.fori_loop(0, n_chunks, scan_bwd, jnp.zeros((D_STATE, GROUP_CH), F32))

    nrm = nrm_ref[...]

    def final_pass(c):
        t0 = pl.multiple_of(c * CHUNK, CHUNK)
        st = jnp.concatenate([sinf_ref[c], sinb_ref[c]], axis=1)
        yo = _dot(cc_ref[pl.ds(t0, CHUNK), :], st)
        ee = _dot(qt_ref[c], exp_ref[...])
        y = (y_ref[pl.ds(t0, CHUNK), :] + yo[:, :GROUP_CH] * ee[:, :GROUP_CH]
             + yo[:, GROUP_CH:] * ee[:, GROUP_CH:])
        y = y * zs_ref[pl.ds(t0, CHUNK), :].astype(F32)
        out_ref[pl.ds(t0, CHUNK), :] = (_rms_scale(y) * nrm).astype(BF16)

    per_chunk_group(final_pass, 2 * SSD_UNROLL)


def _expand_matrix():
    expand = np.zeros((CHUNK, 2 * GROUP_CH), np.float32)
    for r in range(N_PIECES):
        for j in range(HEAD_DIRS):
            expand[Q_E + SUBLANES * r + j, j * HEAD_DIM:(j + 1) * HEAD_DIM] = 1.0
    return jnp.asarray(expand).astype(BF16)


def _ssd(xsc, bt, cc, zs, dtt, bias_rep, alog_rep, dskip_exp, ssd_norm, bsz, seq):
    n_chunks = seq // CHUNK
    assert n_chunks % (2 * SSD_UNROLL) == 0
    expand = _expand_matrix()
    row_spec = lambda width: pl.BlockSpec((None, seq, width), lambda b, g: (b, 0, g))
    par_spec = lambda rows, width: pl.BlockSpec((rows, width), lambda b, g: (0, g))
    return pl.pallas_call(
        functools.partial(_ssd_kernel, seq=seq),
        grid=(bsz, SSD_GROUPS),
        in_specs=[
            row_spec(GROUP_CH),
            pl.BlockSpec((None, None, seq, D_STATE), lambda b, g: (g, b, 0, 0)),
            row_spec(D_STATE), row_spec(GROUP_CH),
            pl.BlockSpec((HEAD_DIRS, seq), lambda b, g: (g, b)),
            pl.BlockSpec((HEAD_DIRS, LANES), lambda b, g: (g, 0)),
            pl.BlockSpec((HEAD_DIRS, LANES), lambda b, g: (g, 0)),
            par_spec(1, GROUP_CH), par_spec(1, GROUP_CH),
            pl.BlockSpec(expand.shape, lambda b, g: (0, 0)),
        ],
        out_specs=pl.BlockSpec((None, seq, GROUP_CH), lambda b, g: (b, 0, g)),
        out_shape=jax.ShapeDtypeStruct((bsz, seq, D_INNER), BF16),
        scratch_shapes=[
            pltpu.VMEM((seq, GROUP_CH), F32),
            pltpu.VMEM((n_chunks, D_STATE, GROUP_CH), F32),
            pltpu.VMEM((n_chunks, D_STATE, GROUP_CH), F32),
            pltpu.VMEM((n_chunks, D_STATE, GROUP_CH), BF16),
            pltpu.VMEM((n_chunks, D_STATE, GROUP_CH), BF16),
            pltpu.VMEM((n_chunks, CHUNK, CHUNK), BF16),
            pltpu.VMEM((2, HEAD_DIRS, seq), F32),
            pltpu.VMEM((n_chunks, GROUP_CH), F32),
            pltpu.VMEM((n_chunks, GROUP_CH), F32),
        ],
        compiler_params=pltpu.CompilerParams(
            dimension_semantics=("arbitrary", "arbitrary"), vmem_limit_bytes=VMEM_LIMIT),
        name="ssd",
    )(xsc.reshape(bsz, seq, D_INNER), bt.reshape(SSD_GROUPS, bsz, seq, D_STATE),
      cc.reshape(bsz, seq, SSD_GROUPS * D_STATE), zs.reshape(bsz, seq, D_INNER), dtt,
      bias_rep, alog_rep, dskip_exp, ssd_norm, expand)


def _dft_kernel(zr_ref, zi_ref, a1_ref, a2_ref, a3_ref, o_ref, sr_ref, si_ref, *, n2):
    cols = zr_ref.shape[-1]
    rows1 = DFT_N1 * BF16_ROWS
    rows2 = n2 * BF16_ROWS

    def stack(r_ref, i_ref, idx, rows):
        return jnp.concatenate([r_ref[idx].reshape(rows, cols), i_ref[idx].reshape(rows, cols)], axis=0)

    for t2 in range(n2):
        y = _dot(a1_ref[...], stack(zr_ref, zi_ref, (slice(None), t2), rows1))
        sr_ref[:, t2] = y[:rows1].astype(BF16).reshape(DFT_N1, BF16_ROWS, cols)
        si_ref[:, t2] = y[rows1:].astype(BF16).reshape(DFT_N1, BF16_ROWS, cols)
    for t3 in range(BF16_ROWS):
        y = _dot(a2_ref[...], stack(sr_ref, si_ref, t3, rows2))
        sr_ref[t3] = y[:rows2].astype(BF16).reshape(n2, BF16_ROWS, cols)
        si_ref[t3] = y[rows2:].astype(BF16).reshape(n2, BF16_ROWS, cols)
    for k2 in range(n2):
        y = _dot(a3_ref[k2], stack(sr_ref, si_ref, (slice(None), k2), rows1))
        o_ref[:, k2] = y.astype(BF16).reshape(BF16_ROWS, BF16_ROWS, cols)


def _dft(zr, zi, consts, bsz, seq):
    a1, a2, a3 = consts
    n2 = seq // (DFT_N1 * BF16_ROWS)
    shape5 = (bsz, DFT_N1, n2, BF16_ROWS, D_FOURIER)
    spec = pl.BlockSpec((None, DFT_N1, n2, BF16_ROWS, DFT_COLS), lambda b, j: (b, 0, 0, 0, j))
    whole = lambda a: pl.BlockSpec(a.shape, lambda b, j: (0,) * a.ndim)
    out = pl.pallas_call(
        functools.partial(_dft_kernel, n2=n2),
        grid=(bsz, D_FOURIER // DFT_COLS),
        in_specs=[spec, spec, whole(a1), whole(a2), whole(a3)],
        out_specs=spec,
        out_shape=jax.ShapeDtypeStruct(shape5, BF16),
        scratch_shapes=[pltpu.VMEM((BF16_ROWS, n2, BF16_ROWS, DFT_COLS), BF16),
                        pltpu.VMEM((BF16_ROWS, n2, BF16_ROWS, DFT_COLS), BF16)],
        compiler_params=pltpu.CompilerParams(
            dimension_semantics=("arbitrary", "arbitrary"), vmem_limit_bytes=VMEM_LIMIT),
        name="dft",
    )(zr.reshape(shape5), zi.reshape(shape5), a1, a2, a3)
    return out.reshape(bsz * seq, D_FOURIER)


def _channel_dft_matrix():
    cidx = np.arange(FOURIER_GROUP_DIM)
    ang = 2.0 * np.pi * np.outer(cidx, cidx) / FOURIER_GROUP_DIM
    cs = np.concatenate([np.cos(ang), -np.sin(ang)], axis=1)
    return jnp.asarray(cs.astype(np.float32)).astype(BF16)


def _dft_constants(seq):
    n1, n3 = DFT_N1, BF16_ROWS
    n2 = seq // (n1 * n3)
    w = lambda n, e: np.exp(-2j * np.pi * (np.asarray(e) % n) / n)
    i16 = np.arange(n3)
    eye = np.eye(n3)
    m1 = np.einsum("kt,ab->aktb", w(n1, np.outer(np.arange(n1), np.arange(n1))), eye).reshape(n3 * n1, n1 * n3)
    k2 = np.arange(n2)
    f2 = w(n2, np.outer(k2, k2))[:, :, None] * w(n1 * n2, np.outer(k2, np.arange(n1)))[None, :, :]
    m2 = np.einsum("ktc,cd->kctd", f2, np.eye(n1)).reshape(n2 * n1, n2 * n1)
    f3 = (w(n3, np.outer(i16, i16))[None, :, :, None]
          * w(n2 * n3, np.outer(k2, i16))[:, None, :, None]
          * w(seq, np.outer(i16, np.arange(n1)))[None, None, :, :])
    m3 = np.einsum("jktc,cd->jkctd", f3, np.eye(n1)).reshape(n2, n3 * n1, n3 * n1)
    m3 = m3 / math.sqrt(seq * FOURIER_GROUP_DIM)
    full = lambda m: np.concatenate([np.concatenate([m.real, -m.imag], axis=-1),
                                     np.concatenate([m.imag, m.real], axis=-1)], axis=-2)
    real_part = lambda m: np.concatenate([m.real, -m.imag], axis=-1)
    to_bf16 = lambda a: jnp.asarray(a.astype(np.float32)).astype(BF16)
    return to_bf16(full(m1)), to_bf16(full(m2)), to_bf16(real_part(m3))


def _merge_kernel(y_ref, mx_ref, gs_ref, x_ref, wssd_ref, wf_ref, bf_ref, wo_ref, o_ref):
    a_out = _dot(y_ref[...], wssd_ref[...])
    f_out = _dot(mx_ref[...], wf_ref[...]) + bf_ref[...]
    gates = gs_ref[...].astype(F32)
    merged = (gates[:, :D_MODEL] * a_out + gates[:, D_MODEL:] * f_out).astype(BF16)
    o_ref[...] = x_ref[...] + _dot(merged, wo_ref[...])


def _merge(y2d, mixed, gs, x2d, w_ssd, w_f, b_f, w_o, bm):
    t = x2d.shape[0]
    const = lambda r, c: pl.BlockSpec((r, c), lambda i: (0, 0), pipeline_mode=pl.Buffered(1))
    return pl.pallas_call(
        _merge_kernel,
        grid=(t // bm,),
        in_specs=[
            pl.BlockSpec((bm, D_INNER), lambda i: (i, 0)),
            pl.BlockSpec((bm, D_FOURIER), lambda i: (i, 0)),
            pl.BlockSpec((bm, 2 * D_MODEL), lambda i: (i, 0)),
            pl.BlockSpec((bm, D_MODEL), lambda i: (i, 0)),
            const(D_INNER, D_MODEL), const(D_FOURIER, D_MODEL), const(1, D_MODEL), const(D_MODEL, D_MODEL),
        ],
        out_specs=pl.BlockSpec((bm, D_MODEL), lambda i: (i, 0)),
        out_shape=jax.ShapeDtypeStruct((t, D_MODEL), F32),
        compiler_params=pltpu.CompilerParams(
            dimension_semantics=("arbitrary",), vmem_limit_bytes=VMEM_LIMIT),
        name="merge",
    )(y2d, mixed, gs, x2d, w_ssd, w_f, b_f, w_o)


def _ffn_kernel(x_ref, g_ref, wg_ref, wu_ref, wd_ref, gfin_ref, o_ref, *, ff_tile):
    x = x_ref[...]
    h = (_rms_scale(x) * g_ref[...]).astype(BF16)
    y = x
    for f in range(D_FF // ff_tile):
        cols = slice(f * ff_tile, (f + 1) * ff_tile)
        act = (_silu_of_twice(_dot(h, wg_ref[:, cols])) * _dot(h, wu_ref[:, cols])).astype(BF16)
        y = y + _dot(act, wd_ref[cols, :])
    o_ref[...] = _rms_scale(y) * gfin_ref[...]


def _ffn(x1, norm_g, w_gate, w_up, w_down, norm_fin, bm, ff_tile):
    t = x1.shape[0]
    resident = lambda a: pl.BlockSpec(a.shape, lambda i: (0, 0), pipeline_mode=pl.Buffered(1))
    return pl.pallas_call(
        functools.partial(_ffn_kernel, ff_tile=ff_tile),
        grid=(t // bm,),
        in_specs=[
            pl.BlockSpec((bm, D_MODEL), lambda i: (i, 0)),
            resident(norm_g), resident(w_gate), resident(w_up), resident(w_down), resident(norm_fin),
        ],
        out_specs=pl.BlockSpec((bm, D_MODEL), lambda i: (i, 0)),
        out_shape=jax.ShapeDtypeStruct((t, D_MODEL), F32),
        compiler_params=pltpu.CompilerParams(
            dimension_semantics=("arbitrary",), vmem_limit_bytes=VMEM_LIMIT),
        name="ffn",
    )(x1, norm_g, w_gate, w_up, w_down, norm_fin)


def _prep_weights(norm_mix, w_in, conv_w, conv_b, dt_bias_f, dt_bias_b, a_log_f, a_log_b, d_skip,
                  ssd_norm, w_ssd_out, w_fourier_out, b_fourier_out, w_out, norm_ffn, w_gate_up,
                  w_down, norm_final):
    o_dt = D_INNER + (D_INNER + 2 * SSD_GROUPS * D_STATE)
    o_u = o_dt + 2 * SSD_HEADS
    o_g = o_u + D_FOURIER
    w_all = jnp.concatenate([0.5 * w_in[:, :D_INNER], w_in[:, D_INNER:o_dt], 0.5 * w_in[:, o_g:],
                             w_in[:, o_u:o_g]], axis=1).astype(BF16)

    def by_group(f, b):
        return jnp.concatenate([f.reshape(SSD_GROUPS, HEADS_PER_GROUP),
                                b.reshape(SSD_GROUPS, HEADS_PER_GROUP)], axis=1).reshape(-1)

    w_dt_cols = w_in[:, o_dt:o_u]
    w_dt = jnp.concatenate([w_dt_cols[:, :SSD_HEADS].reshape(D_MODEL, SSD_GROUPS, HEADS_PER_GROUP),
                            w_dt_cols[:, SSD_HEADS:].reshape(D_MODEL, SSD_GROUPS, HEADS_PER_GROUP)],
                           axis=2).reshape(D_MODEL, 2 * SSD_HEADS)
    w_dt = jnp.pad(w_dt, ((0, 0), (0, LANES - 2 * SSD_HEADS))).astype(BF16)
    rep = lambda v: jnp.broadcast_to(v.astype(F32)[:, None], (2 * SSD_HEADS, LANES))
    return dict(
        norm_mix=norm_mix.reshape(1, D_MODEL), w_all=w_all, w_dt=w_dt,
        conv_w=0.5 * conv_w, conv_b=0.5 * conv_b.reshape(1, -1),
        bias_rep=rep(by_group(dt_bias_f, dt_bias_b)), alog_rep=rep(by_group(a_log_f, a_log_b)),
        dskip_exp=jnp.repeat(d_skip.astype(F32), HEAD_DIM).reshape(1, D_INNER),
        ssd_norm=ssd_norm.reshape(1, D_INNER),
        w_ssd=w_ssd_out.astype(BF16), w_f=w_fourier_out.astype(BF16),
        b_f=b_fourier_out.reshape(1, D_MODEL), w_o=w_out.astype(BF16),
        norm_ffn=norm_ffn.reshape(1, D_MODEL),
        w_gate=(0.5 * w_gate_up[:, :D_FF]).astype(BF16), w_up=w_gate_up[:, D_FF:].astype(BF16),
        w_down=w_down.astype(BF16), norm_final=norm_final.reshape(1, D_MODEL),
    )


def _trunk(x, p):
    bsz, seq, _ = x.shape
    t = bsz * seq
    bm = min(1024, t)
    x2d = x.reshape(t, D_MODEL)
    zs, xsc, bt, cc, gs, zr, zi, dtt = _inproj(x2d, p["norm_mix"], p["w_all"], p["w_dt"],
                                               _channel_dft_matrix(), p["conv_w"], p["conv_b"], bm, seq)
    y = _ssd(xsc, bt, cc, zs, dtt, p["bias_rep"], p["alog_rep"], p["dskip_exp"], p["ssd_norm"], bsz, seq)
    mixed = _dft(zr, zi, _dft_constants(seq), bsz, seq)
    x1 = _merge(y.reshape(t, D_INNER), mixed, gs, x2d, p["w_ssd"], p["w_f"], p["b_f"], p["w_o"], bm)
    out = _ffn(x1, p["norm_ffn"], p["w_gate"], p["w_up"], p["w_down"], p["norm_final"], min(512, t),
               D_FF // 2)
    return out.reshape(bsz, seq, D_MODEL)


def kernel(x_prompt, x_sample, norm_mix, w_in, conv_w, conv_b, dt_bias_f, dt_bias_b, a_log_f, a_log_b,
           d_skip, ssd_norm, w_ssd_out, w_fourier_out, b_fourier_out, w_out, norm_ffn, w_gate_up, w_down,
           norm_final):
    p = _prep_weights(norm_mix[0], w_in[0], conv_w[0], conv_b[0], dt_bias_f[0], dt_bias_b[0],
                      a_log_f[0], a_log_b[0], d_skip[0], ssd_norm[0], w_ssd_out[0], w_fourier_out[0],
                      b_fourier_out[0], w_out[0], norm_ffn[0], w_gate_up[0], w_down[0], norm_final)
    return (_trunk(x_prompt, p), _trunk(x_sample, p))
```
